```python
import math
import jax, jax.numpy as jnp
from jax import lax
import numpy as np

D_MODEL = 2048
BATCH = 8
SEQ = 4096
DEPTH = 4

CHUNK = 64
Q_BLOCK = 128
MLA_HEADS = 16
QK_NOPE_DIM = 128
QK_ROPE_DIM = 64
V_HEAD_DIM = 128
Q_LORA_RANK = 512
KV_LORA_RANK = 256
ROPE_THETA = 10000.0
SSM_WIDTH = D_MODEL // 2
SSM_GROUP = 16
SSM_GROUPS = SSM_WIDTH // SSM_GROUP
SSM_STATE = 64
DT_MIN = 1e-3
DT_MAX = 1e-1
FFN_HIDDEN = -(-8 * D_MODEL // (3 * 256)) * 256
OFF_KV = Q_LORA_RANK
OFF_SSM = OFF_KV + KV_LORA_RANK + QK_ROPE_DIM
OFF_GATE = OFF_SSM + SSM_WIDTH
IN_WIDTH = OFF_GATE + 2 * D_MODEL
EPS = 1e-6

kernel_name = "chunk_causal_mla_s5_hybrid"


def rms_norm(x, w):
    xf = x.astype(jnp.float32)
    y = xf * lax.rsqrt(jnp.mean(xf * xf, axis=-1, keepdims=True) + EPS)
    return (y * w.astype(jnp.float32)).astype(x.dtype)


def rope_tables(positions):
    inv_freq = ROPE_THETA ** (-jnp.arange(0, QK_ROPE_DIM, 2, dtype=jnp.float32) / QK_ROPE_DIM)
    ang = positions.astype(jnp.float32)[..., None] * inv_freq
    return jnp.cos(ang), jnp.sin(ang)


def apply_rope(x, cos, sin):
    half = x.shape[-1] // 2
    x1 = x[..., :half].astype(jnp.float32)
    x2 = x[..., half:].astype(jnp.float32)
    return jnp.concatenate([x1 * cos - x2 * sin, x1 * sin + x2 * cos], axis=-1).astype(x.dtype)


def mla(cq_raw, ckv_raw, q_norm_w, kv_norm_w, w_uq, w_ukv, w_o, cos, sin):
    B, S, _ = cq_raw.shape
    H = MLA_HEADS
    c_q = rms_norm(cq_raw, q_norm_w)
    q = jnp.einsum('bsr,re->bse', c_q, w_uq).reshape(B, S, H, QK_NOPE_DIM + QK_ROPE_DIM)
    q_nope = q[..., :QK_NOPE_DIM]
    q_pe = apply_rope(q[..., QK_NOPE_DIM:], cos[:, :, None, :], sin[:, :, None, :])
    c_kv = rms_norm(ckv_raw[..., :KV_LORA_RANK], kv_norm_w)
    k_pe = apply_rope(ckv_raw[..., KV_LORA_RANK:], cos, sin)
    kv = jnp.einsum('bsr,re->bse', c_kv, w_ukv).reshape(B, S, H, QK_NOPE_DIM + V_HEAD_DIM)
    k_nope = kv[..., :QK_NOPE_DIM]
    v = kv[..., QK_NOPE_DIM:]
    scale = (QK_NOPE_DIM + QK_ROPE_DIM) ** -0.5
    outs = []
    for q0 in range(0, S, Q_BLOCK):
        q1 = min(q0 + Q_BLOCK, S)
        kv_len = min(S, ((q1 - 1) // CHUNK + 1) * CHUNK)
        s = (jnp.einsum('bqhd,bkhd->bhqk', q_nope[:, q0:q1], k_nope[:, :kv_len],
                        preferred_element_type=jnp.float32)
             + jnp.einsum('bqhd,bkd->bhqk', q_pe[:, q0:q1], k_pe[:, :kv_len],
                          preferred_element_type=jnp.float32)) * scale
        q_chunk = jnp.arange(q0, q1) // CHUNK
        k_chunk = jnp.arange(kv_len) // CHUNK
        mask = k_chunk[None, :] <= q_chunk[:, None]
        s = jnp.where(mask, s, jnp.finfo(jnp.float32).min)
        p = jax.nn.softmax(s, axis=-1).astype(v.dtype)
        outs.append(jnp.einsum('bhqk,bkhd->bqhd', p, v[:, :kv_len]))
    o = jnp.concatenate(outs, axis=1).reshape(B, S, H * V_HEAD_DIM)
    return jnp.einsum('bse,ed->bsd', o, w_o)


def s5(u, a_re, a_im, log_dt, b_re, b_im, c_re, c_im, d_skip, w_glu, b_glu):
    B, S, _ = u.shape
    G, N, P = SSM_GROUPS, SSM_STATE, SSM_GROUP
    f32 = jnp.float32
    uf = u.astype(f32).reshape(B, S, G, P)
    a_re = a_re.astype(f32)
    a_im = a_im.astype(f32)
    dt = jnp.exp(log_dt.astype(f32))[:, None]
    mag = jnp.exp(a_re * dt)
    abar_re = mag * jnp.cos(a_im * dt)
    abar_im = mag * jnp.sin(a_im * dt)
    den = a_re * a_re + a_im * a_im
    nr = abar_re - 1.0
    f_re = (nr * a_re + abar_im * a_im) / den
    f_im = (abar_im * a_re - nr * a_im) / den
    br = b_re.astype(f32)
    bi = b_im.astype(f32)
    bb_re = f_re[..., None] * br - f_im[..., None] * bi
    bb_im = f_re[..., None] * bi + f_im[..., None] * br
    bu_re = jnp.einsum('bsgp,gnp->bsgn', uf, bb_re)
    bu_im = jnp.einsum('bsgp,gnp->bsgn', uf, bb_im)
    ar = jnp.broadcast_to(abar_re[None, None], (1, S, G, N))
    ai = jnp.broadcast_to(abar_im[None, None], (1, S, G, N))

    def combine(e1, e2):
        a1r, a1i, b1r, b1i = e1
        a2r, a2i, b2r, b2i = e2
        return (a2r * a1r - a2i * a1i,
                a2r * a1i + a2i * a1r,
                a2r * b1r - a2i * b1i + b2r,
                a2r * b1i + a2i * b1r + b2i)

    _, _, xr, xi = lax.associative_scan(combine, (ar, ai, bu_re, bu_im), axis=1)
    y = (jnp.einsum('bsgn,gpn->bsgp', xr, c_re.astype(f32))
         - jnp.einsum('bsgn,gpn->bsgp', xi, c_im.astype(f32))
         + d_skip.astype(f32).reshape(G, P) * uf).reshape(B, S, SSM_WIDTH)
    z = jax.nn.gelu(y).astype(u.dtype)
    zz = jnp.einsum('bsw,we->bse', z, w_glu) + b_glu
    return zz[..., :D_MODEL] * jax.nn.sigmoid(zz[..., D_MODEL:])


def setup_inputs(seed: int = 0) -> dict:
    key = jax.random.key(seed)
    ks = jax.random.split(key, 32)
    L, D = DEPTH, D_MODEL
    G, N, P = SSM_GROUPS, SSM_STATE, SSM_GROUP
    f32 = jnp.float32

    def nrm(k, shape, scale):
        return jax.random.normal(k, shape, f32) * scale

    def gain(k, n):
        return 1.0 + 0.05 * jax.random.normal(k, (L, n), f32)

    x = jax.random.normal(ks[0], (BATCH, SEQ, D), f32)
    offsets = jax.random.randint(ks[1], (BATCH, 1), 0, 4096, dtype=jnp.int32)
    positions = offsets + jnp.arange(SEQ, dtype=jnp.int32)[None, :]
    n_idx = jnp.arange(N, dtype=f32)
    return {
        "x": x,
        "positions": positions,
        "pre_mix_norm": gain(ks[2], D),
        "w_in": nrm(ks[3], (L, D, IN_WIDTH), D ** -0.5),
        "b_gate": nrm(ks[4], (L, 2 * D), 0.01),
        "q_norm": gain(ks[5], Q_LORA_RANK),
        "kv_norm": gain(ks[6], KV_LORA_RANK),
        "w_uq": nrm(ks[7], (L, Q_LORA_RANK, MLA_HEADS * (QK_NOPE_DIM + QK_ROPE_DIM)), Q_LORA_RANK ** -0.5),
        "w_ukv": nrm(ks[8], (L, KV_LORA_RANK, MLA_HEADS * (QK_NOPE_DIM + V_HEAD_DIM)), KV_LORA_RANK ** -0.5),
        "w_o_mla": nrm(ks[9], (L, MLA_HEADS * V_HEAD_DIM, D), (MLA_HEADS * V_HEAD_DIM) ** -0.5),
        "ssm_a_re": -0.5 + nrm(ks[10], (L, G, N), 0.01),
        "ssm_a_im": math.pi * n_idx + nrm(ks[11], (L, G, N), 0.01),
        "ssm_log_dt": jax.random.uniform(ks[12], (L, G), f32, math.log(DT_MIN), math.log(DT_MAX)),
        "ssm_b_re": nrm(ks[13], (L, G, N, P), (2 * P) ** -0.5),
        "ssm_b_im": nrm(ks[14], (L, G, N, P), (2 * P) ** -0.5),
        "ssm_c_re": nrm(ks[15], (L, G, P, N), (2 * N) ** -0.5),
        "ssm_c_im": nrm(ks[16], (L, G, P, N), (2 * N) ** -0.5),
        "ssm_d": nrm(ks[17], (L, SSM_WIDTH), 1.0),
        "w_glu": nrm(ks[18], (L, SSM_WIDTH, 2 * D), SSM_WIDTH ** -0.5),
        "b_glu": nrm(ks[19], (L, 2 * D), 0.01),
        "w_out": nrm(ks[20], (L, D, D), D ** -0.5),
        "post_mix_norm": gain(ks[21], D),
        "pre_ffn_norm": gain(ks[22], D),
        "w_ffn_gate": nrm(ks[23], (L, D, FFN_HIDDEN), D ** -0.5),
        "w_ffn_up": nrm(ks[24], (L, D, FFN_HIDDEN), D ** -0.5),
        "w_ffn_down": nrm(ks[25], (L, FFN_HIDDEN, D), FFN_HIDDEN ** -0.5),
        "post_ffn_norm": gain(ks[26], D),
    }


def reference(x, positions, pre_mix_norm, w_in, b_gate, q_norm, kv_norm, w_uq, w_ukv, w_o_mla,
              ssm_a_re, ssm_a_im, ssm_log_dt, ssm_b_re, ssm_b_im, ssm_c_re, ssm_c_im, ssm_d,
              w_glu, b_glu, w_out, post_mix_norm, pre_ffn_norm, w_ffn_gate, w_ffn_up,
              w_ffn_down, post_ffn_norm):
    cos, sin = rope_tables(positions)
    for l in range(DEPTH):
        h = rms_norm(x, pre_mix_norm[l])
        proj = jnp.einsum('bsd,de->bse', h, w_in[l])
        cq_raw = proj[..., :OFF_KV]
        ckv_raw = proj[..., OFF_KV:OFF_SSM]
        u = proj[..., OFF_SSM:OFF_GATE]
        gates = jax.nn.sigmoid((proj[..., OFF_GATE:] + b_gate[l]).astype(jnp.float32))
        a_out = mla(cq_raw, ckv_raw, q_norm[l], kv_norm[l], w_uq[l], w_ukv[l], w_o_mla[l], cos, sin)
        s_out = s5(u, ssm_a_re[l], ssm_a_im[l], ssm_log_dt[l], ssm_b_re[l], ssm_b_im[l],
                   ssm_c_re[l], ssm_c_im[l], ssm_d[l], w_glu[l], b_glu[l])
        merged = (gates[..., :D_MODEL] * a_out.astype(jnp.float32)
                  + gates[..., D_MODEL:] * s_out.astype(jnp.float32)).astype(x.dtype)
        mix = jnp.einsum('bsd,de->bse', merged, w_out[l])
        x = x + rms_norm(mix, post_mix_norm[l])
        h = rms_norm(x, pre_ffn_norm[l])
        f = jnp.einsum('bsf,fd->bsd',
                       jax.nn.silu(jnp.einsum('bsd,df->bsf', h, w_ffn_gate[l]))
                       * jnp.einsum('bsd,df->bsf', h, w_ffn_up[l]),
                       w_ffn_down[l])
        x = x + rms_norm(f, post_ffn_norm[l])
    return x
```

```python
import functools
import math

import jax
import jax.numpy as jnp
from jax import lax
from jax.experimental import pallas as pl
from jax.experimental.pallas import tpu as pltpu

F32 = jnp.float32
BF16 = jnp.bfloat16

CHUNK = 64
MLA_HEADS = 16
QK_NOPE_DIM = 128
QK_ROPE_DIM = 64
V_HEAD_DIM = 128
Q_LORA_RANK = 512
KV_LORA_RANK = 256
ROPE_THETA = 10000.0
SSM_GROUP = 16
SSM_STATE = 64
EPS = 1e-6

LANES = 128
HEAD_PAD = 256
SSM_CHUNK = 16
SSM_ROW = SSM_CHUNK * SSM_GROUP
VMEM_LIMIT = 56 * 1024 * 1024
NEG_BIG = -1e30

SMALL_W = 1024
OFF_KPE = Q_LORA_RANK + KV_LORA_RANK


def _cparams(*sem):
    return pltpu.CompilerParams(dimension_semantics=sem, vmem_limit_bytes=VMEM_LIMIT)


def _rms(x, w):
    return x * lax.rsqrt(jnp.mean(x * x, axis=-1, keepdims=True) + EPS) * w


def _rope_table_kernel(pos_ref, freq_ref, tab_ref):
    ang = pos_ref[...].astype(F32) * freq_ref[...]
    lane = lax.broadcasted_iota(jnp.int32, ang.shape, 1)
    c = jnp.cos(ang)
    s = jnp.sin(ang)
    tab_ref[...] = jnp.where(lane < 64, c, jnp.where(lane < 96, -s, s))


def _rope_table(positions):
    t = positions.size
    tm = min(t, 2048)
    half = QK_ROPE_DIM // 2
    inv_freq = ROPE_THETA ** (-jnp.arange(0, QK_ROPE_DIM, 2, dtype=F32) / QK_ROPE_DIM)
    freq = jnp.tile(inv_freq, LANES // half).reshape(1, LANES)
    return pl.pallas_call(
        _rope_table_kernel,
        out_shape=jax.ShapeDtypeStruct((t, LANES), F32),
        grid=(t // tm,),
        in_specs=[pl.BlockSpec((tm, 1), lambda i: (i, 0)),
                  pl.BlockSpec((1, LANES), lambda i: (0, 0))],
        out_specs=pl.BlockSpec((tm, LANES), lambda i: (i, 0)),
        compiler_params=_cparams("parallel"),
        name="rope_table",
    )(positions.reshape(t, 1), freq)


def _in_proj_kernel(x_ref, nw_ref, w_ref, b_ref, o_ref, h_ref, *, plain_tiles):
    j = pl.program_id(1)

    @pl.when(j == 0)
    def _():
        h_ref[...] = _rms(x_ref[...], nw_ref[...]).astype(BF16)

    acc = jnp.dot(h_ref[...], w_ref[...], preferred_element_type=F32)

    @pl.when(j < plain_tiles)
    def _():
        o_ref[...] = acc.astype(o_ref.dtype)

    @pl.when(j >= plain_tiles)
    def _():
        o_ref[...] = jax.nn.sigmoid(acc + b_ref[...]).astype(o_ref.dtype)


def _in_proj(x, norm_w, w, bias, *, gate_off):
    t, d = x.shape
    n = w.shape[1]
    tm = min(t, 1024)
    tn = 1024
    return pl.pallas_call(
        functools.partial(_in_proj_kernel, plain_tiles=gate_off // tn),
        out_shape=jax.ShapeDtypeStruct((t, n), BF16),
        grid=(t // tm, n // tn),
        in_specs=[pl.BlockSpec((tm, d), lambda i, j: (i, 0)),
                  pl.BlockSpec((1, d), lambda i, j: (0, 0)),
                  pl.BlockSpec((d, tn), lambda i, j: (0, j)),
                  pl.BlockSpec((1, tn), lambda i, j: (0, j))],
        out_specs=pl.BlockSpec((tm, tn), lambda i, j: (i, j)),
        scratch_shapes=[pltpu.VMEM((tm, d), BF16)],
        compiler_params=_cparams("parallel", "arbitrary"),
        name="in_proj",
    )(x, norm_w, w, bias)


def _qkv_up_kernel(p_ref, tab_ref, qn_ref, kvn_ref, wq_ref, wk_ref, wv_ref,
                   q_ref, k_ref, v_ref, *, scale):
    tab = tab_ref[...]
    tabs = tab * scale
    cq = _rms(p_ref[:, 0:Q_LORA_RANK].astype(F32), qn_ref[...]).astype(BF16)
    q_all = jnp.dot(cq, wq_ref[...], preferred_element_type=F32)
    for h in range(MLA_HEADS):
        base = h * HEAD_PAD
        q_ref[h, :, 0:QK_NOPE_DIM] = (q_all[:, base:base + QK_NOPE_DIM] * scale).astype(BF16)
        q_ref[h, :, QK_NOPE_DIM:HEAD_PAD] = (
            q_all[:, base + QK_NOPE_DIM:base + HEAD_PAD] * tabs).astype(BF16)

    ckv = _rms(p_ref[:, Q_LORA_RANK:OFF_KPE].astype(F32), kvn_ref[...]).astype(BF16)
    k_all = jnp.dot(ckv, wk_ref[...], preferred_element_type=F32)
    v_all = jnp.dot(ckv, wv_ref[...], preferred_element_type=F32)
    kt = p_ref[:, OFF_KPE:OFF_KPE + LANES].astype(F32) * tab
    kr = (kt + pltpu.roll(kt, LANES // 2, 1)).astype(BF16)
    ones = jnp.ones((kr.shape[0], LANES), BF16)
    for h in range(MLA_HEADS):
        k_ref[h, :, 0:QK_NOPE_DIM] = k_all[:, h * QK_NOPE_DIM:(h + 1) * QK_NOPE_DIM].astype(BF16)
        k_ref[h, :, QK_NOPE_DIM:HEAD_PAD] = kr
        v_ref[h, :, 0:V_HEAD_DIM] = v_all[:, h * V_HEAD_DIM:(h + 1) * V_HEAD_DIM].astype(BF16)
        v_ref[h, :, V_HEAD_DIM:HEAD_PAD] = ones


def _qkv_up(proj, tab, q_norm, kv_norm, wq, wk, wv, *, batch, seq):
    tm = min(seq, 512)
    nb = seq // tm
    hshape = jax.ShapeDtypeStruct((batch, MLA_HEADS, seq, HEAD_PAD), BF16)
    hspec = pl.BlockSpec((None, MLA_HEADS, tm, HEAD_PAD), lambda b, i: (b, 0, i, 0))
    scale = (QK_NOPE_DIM + QK_ROPE_DIM) ** -0.5
    const = lambda b, i: (0, 0)
    return pl.pallas_call(
        functools.partial(_qkv_up_kernel, scale=scale),
        out_shape=(hshape, hshape, hshape),
        grid=(batch, nb),
        in_specs=[pl.BlockSpec((tm, SMALL_W), lambda b, i: (b * nb + i, 0)),
                  pl.BlockSpec((tm, LANES), lambda b, i: (b * nb + i, 0)),
                  pl.BlockSpec((1, Q_LORA_RANK), const),
                  pl.BlockSpec((1, KV_LORA_RANK), const),
                  pl.BlockSpec(wq.shape, const),
                  pl.BlockSpec(wk.shape, const),
                  pl.BlockSpec(wv.shape, const)],
        out_specs=(hspec, hspec, hspec),
        compiler_params=_cparams("parallel", "parallel"),
        name="qkv_up",
    )(proj, tab, q_norm, kv_norm, wq, wk, wv)


def _attn_kernel(q_ref, k_ref, v_ref, o_ref, m_ref, acc_ref, *, tq):
    i = pl.program_id(2)
    q = q_ref[...]
    m_ref[...] = jnp.full(m_ref.shape, NEG_BIG, F32)
    acc_ref[...] = jnp.zeros(acc_ref.shape, F32)

    def block(kb, masked):
        rows = pl.ds(pl.multiple_of(kb * tq, tq), tq)
        s = lax.dot_general(q, k_ref[rows, :], (((1,), (1,)), ((), ())),
                            preferred_element_type=F32)
        if masked:
            qc = lax.broadcasted_iota(jnp.int32, s.shape, 0) // CHUNK
            kc = lax.broadcasted_iota(jnp.int32, s.shape, 1) // CHUNK
            s = jnp.where(kc <= qc, s, NEG_BIG)
        m_prev = m_ref[:, 0:1]
        m_new = jnp.maximum(m_prev, jnp.max(s, axis=1, keepdims=True))
        alpha = jnp.exp(m_prev - m_new)
        p = jnp.exp(s - m_new).astype(BF16)
        acc_ref[...] = alpha * acc_ref[...] + jnp.dot(p, v_ref[rows, :],
                                                      preferred_element_type=F32)
        m_ref[...] = jnp.broadcast_to(m_new, m_ref.shape)

    def full_block(kb, carry):
        block(kb, False)
        return carry

    lax.fori_loop(0, i, full_block, 0)
    block(i, True)
    acc = acc_ref[...]
    o_ref[...] = (acc[:, 0:V_HEAD_DIM] / acc[:, V_HEAD_DIM:HEAD_PAD]).astype(o_ref.dtype)


def _attention(q, k, v):
    batch, heads, seq, _ = q.shape
    tq = min(seq, 512)
    kv_spec = pl.BlockSpec((None, None, seq, HEAD_PAD), lambda b, h, i: (b, h, 0, 0))
    return pl.pallas_call(
        functools.partial(_attn_kernel, tq=tq),
        out_shape=jax.ShapeDtypeStruct((batch, seq, heads * V_HEAD_DIM), BF16),
        grid=(batch, heads, seq // tq),
        in_specs=[pl.BlockSpec((None, None, tq, HEAD_PAD), lambda b, h, i: (b, h, i, 0)),
                  kv_spec, kv_spec],
        out_specs=pl.BlockSpec((None, tq, V_HEAD_DIM), lambda b, h, i: (b, i, h)),
        scratch_shapes=[pltpu.VMEM((tq, LANES), F32), pltpu.VMEM((tq, HEAD_PAD), F32)],
        compiler_params=_cparams("parallel", "parallel", "arbitrary"),
        name="attn",
    )(q, k, v)


def _ssm_weights(a_re, a_im, log_dt, b_re, b_im, c_re, c_im, d_skip):
    hi = lax.Precision.HIGHEST
    L = SSM_CHUNK
    g, n = a_re.shape
    p = SSM_GROUP
    dt = jnp.exp(log_dt)[:, None]
    mag = jnp.exp(a_re * dt)
    abar_re = mag * jnp.cos(a_im * dt)
    abar_im = mag * jnp.sin(a_im * dt)
    den = a_re * a_re + a_im * a_im
    nr = abar_re - 1.0
    f_re = (nr * a_re + abar_im * a_im) / den
    f_im = (abar_im * a_re - nr * a_im) / den
    bb_re = f_re[..., None] * b_re - f_im[..., None] * b_im
    bb_im = f_re[..., None] * b_im + f_im[..., None] * b_re
    kk = jnp.arange(L + 1, dtype=F32)[:, None, None]
    pmag = jnp.exp(kk * (a_re * dt))
    pw_re = pmag * jnp.cos(kk * (a_im * dt))
    pw_im = pmag * jnp.sin(kk * (a_im * dt))
    ca_re = c_re[None] * pw_re[:, :, None, :] - c_im[None] * pw_im[:, :, None, :]
    ca_im = c_re[None] * pw_im[:, :, None, :] + c_im[None] * pw_re[:, :, None, :]
    kern = (jnp.einsum('tgpn,gnq->tgpq', ca_re[:L], bb_re, precision=hi)
            - jnp.einsum('tgpn,gnq->tgpq', ca_im[:L], bb_im, precision=hi))
    kern = jnp.concatenate([kern, jnp.zeros((1, g, p, p), F32)], axis=0)
    ii = jnp.arange(L)
    lag = ii[None, :] - ii[:, None]
    lag = jnp.where(lag >= 0, lag, L)
    toep = kern[lag]
    toep = toep.transpose(2, 0, 4, 1, 3).reshape(g, L * p, L * p)
    rev_re = pw_re[L - 1 - ii]
    rev_im = pw_im[L - 1 - ii]
    w_re = rev_re[:, :, :, None] * bb_re[None] - rev_im[:, :, :, None] * bb_im[None]
    w_im = rev_re[:, :, :, None] * bb_im[None] + rev_im[:, :, :, None] * bb_re[None]
    w_re = w_re.transpose(1, 0, 3, 2).reshape(g, L * p, n)
    w_im = w_im.transpose(1, 0, 3, 2).reshape(g, L * p, n)
    z_re = ca_re[1:].transpose(1, 3, 0, 2).reshape(g, n, L * p)
    z_im = -ca_im[1:].transpose(1, 3, 0, 2).reshape(g, n, L * p)
    d_row = jnp.tile(d_skip.reshape(g, 1, p), (1, 1, L))
    return (toep.astype(BF16), w_re.astype(BF16), w_im.astype(BF16), z_re.astype(BF16),
            z_im.astype(BF16), pw_re[L].reshape(g, 1, n), pw_im[L].reshape(g, 1, n), d_row)


def _ssm_kernel(u_ref, toep_ref, wr_ref, wi_ref, zr_ref, zi_ref, ar_ref, ai_ref, d_ref,
                o_ref, hr_ref, hi_ref, *, groups, batch, chunks):
    for g in range(groups):
        u = u_ref[g]
        hr_ref[g] = jnp.dot(u, wr_ref[g], preferred_element_type=F32)
        hi_ref[g] = jnp.dot(u, wi_ref[g], preferred_element_type=F32)

    ar = jnp.broadcast_to(ar_ref[...], (groups, batch, SSM_STATE))
    ai = jnp.broadcast_to(ai_ref[...], (groups, batch, SSM_STATE))

    def step(c, carry):
        hr, hi = carry
        rows = pl.ds(pl.multiple_of(c * batch, batch), batch)
        vr = hr_ref[:, rows, :]
        vi = hi_ref[:, rows, :]
        hr_ref[:, rows, :] = hr
        hi_ref[:, rows, :] = hi
        return ar * hr - ai * hi + vr, ar * hi + ai * hr + vi

    zero = jnp.zeros((groups, batch, SSM_STATE), F32)
    lax.fori_loop(0, chunks, step, (zero, zero))

    for g in range(groups):
        u = u_ref[g]
        y = (jnp.dot(u, toep_ref[g], preferred_element_type=F32)
             + jnp.dot(hr_ref[g].astype(BF16), zr_ref[g], preferred_element_type=F32)
             + jnp.dot(hi_ref[g].astype(BF16), zi_ref[g], preferred_element_type=F32)
             + u.astype(F32) * d_ref[g])
        o_ref[g] = jax.nn.gelu(y).astype(o_ref.dtype)


def _ssm(u_t, weights, *, batch):
    toep, w_re, w_im, z_re, z_im, al_re, al_im, d_row = weights
    g, m, row = u_t.shape
    gb = 2
    chunks = m // batch

    def gspec(a):
        return pl.BlockSpec((gb,) + a.shape[1:], lambda i: (i, 0, 0))

    return pl.pallas_call(
        functools.partial(_ssm_kernel, groups=gb, batch=batch, chunks=chunks),
        out_shape=jax.ShapeDtypeStruct(u_t.shape, BF16),
        grid=(g // gb,),
        in_specs=[gspec(u_t), gspec(toep), gspec(w_re), gspec(w_im), gspec(z_re), gspec(z_im),
                  gspec(al_re), gspec(al_im), gspec(d_row)],
        out_specs=gspec(u_t),
        scratch_shapes=[pltpu.VMEM((gb, m, SSM_STATE), F32), pltpu.VMEM((gb, m, SSM_STATE), F32)],
        compiler_params=_cparams("parallel"),
        name="ssm",
    )(u_t, toep, w_re, w_im, z_re, z_im, al_re, al_im, d_row)


def _merge_kernel(o_ref, z_ref, ga_ref, gb_ref, wo_ref, w1_ref, w2_ref, b1_ref, b2_ref, m_ref):
    a = jnp.dot(o_ref[...], wo_ref[...], preferred_element_type=F32)
    z = z_ref[...]
    z1 = jnp.dot(z, w1_ref[...], preferred_element_type=F32) + b1_ref[...]
    z2 = jnp.dot(z, w2_ref[...], preferred_element_type=F32) + b2_ref[...]
    s = z1 * jax.nn.sigmoid(z2)
    m_ref[...] = (ga_ref[...].astype(F32) * a + gb_ref[...].astype(F32) * s).astype(m_ref.dtype)


def _merge(o, z, proj, w_o, w_glu, b_glu, *, gate_off):
    t, d = o.shape
    tm = min(t, 1024)
    tn = 512
    nc = d // tn
    ga0 = gate_off // tn
    return pl.pallas_call(
        _merge_kernel,
        out_shape=jax.ShapeDtypeStruct((t, d), BF16),
        grid=(t // tm, nc),
        in_specs=[pl.BlockSpec((tm, d), lambda i, j: (i, 0)),
                  pl.BlockSpec((tm, z.shape[1]), lambda i, j: (i, 0)),
                  pl.BlockSpec((tm, tn), lambda i, j: (i, ga0 + j)),
                  pl.BlockSpec((tm, tn), lambda i, j: (i, ga0 + nc + j)),
                  pl.BlockSpec((d, tn), lambda i, j: (0, j)),
                  pl.BlockSpec((z.shape[1], tn), lambda i, j: (0, j)),
                  pl.BlockSpec((z.shape[1], tn), lambda i, j: (0, nc + j)),
                  pl.BlockSpec((1, tn), lambda i, j: (0, j)),
                  pl.BlockSpec((1, tn), lambda i, j: (0, nc + j))],
        out_specs=pl.BlockSpec((tm, tn), lambda i, j: (i, j)),
        compiler_params=_cparams("parallel", "arbitrary"),
        name="merge",
    )(o, z, proj, proj, w_o, w_glu, w_glu, b_glu, b_glu)


def _out_proj_kernel(m_ref, x_ref, w_ref, nw_ref, o_ref):
    mix = jnp.dot(m_ref[...], w_ref[...], preferred_element_type=F32)
    o_ref[...] = x_ref[...] + _rms(mix, nw_ref[...])


def _out_proj(merged, x, w_out, norm_w):
    t, d = x.shape
    tm = min(t, 512)
    return pl.pallas_call(
        _out_proj_kernel,
        out_shape=jax.ShapeDtypeStruct((t, d), F32),
        grid=(t // tm,),
        in_specs=[pl.BlockSpec((tm, d), lambda i: (i, 0)),
                  pl.BlockSpec((tm, d), lambda i: (i, 0)),
                  pl.BlockSpec((d, d), lambda i: (0, 0)),
                  pl.BlockSpec((1, d), lambda i: (0, 0))],
        out_specs=pl.BlockSpec((tm, d), lambda i: (i, 0)),
        compiler_params=_cparams("parallel"),
        name="out_proj",
    )(merged, x, w_out, norm_w)


def _ffn_kernel(x_ref, nw_ref, wg_ref, wu_ref, wd_ref, pw_ref, o_ref, h_ref, acc_ref):
    j = pl.program_id(1)

    @pl.when(j == 0)
    def _():
        h_ref[...] = _rms(x_ref[...], nw_ref[...]).astype(BF16)
        acc_ref[...] = jnp.zeros(acc_ref.shape, F32)

    h = h_ref[...]
    gate = jnp.dot(h, wg_ref[...], preferred_element_type=F32)
    up = jnp.dot(h, wu_ref[...], preferred_element_type=F32)
    act = (jax.nn.silu(gate) * up).astype(BF16)
    acc_ref[...] += jnp.dot(act, wd_ref[...], preferred_element_type=F32)

    @pl.when(j == pl.num_programs(1) - 1)
    def _():
        o_ref[...] = x_ref[...] + _rms(acc_ref[...], pw_ref[...])


def _ffn(x, pre_w, w_gate, w_up, w_down, post_w):
    t, d = x.shape
    f = w_gate.shape[1]
    tm = min(t, 512)
    tf = 512
    return pl.pallas_call(
        _ffn_kernel,
        out_shape=jax.ShapeDtypeStruct((t, d), F32),
        grid=(t // tm, f // tf),
        in_specs=[pl.BlockSpec((tm, d), lambda i, j: (i, 0)),
                  pl.BlockSpec((1, d), lambda i, j: (0, 0)),
                  pl.BlockSpec((d, tf), lambda i, j: (0, j)),
                  pl.BlockSpec((d, tf), lambda i, j: (0, j)),
                  pl.BlockSpec((tf, d), lambda i, j: (j, 0)),
                  pl.BlockSpec((1, d), lambda i, j: (0, 0))],
        out_specs=pl.BlockSpec((tm, d), lambda i, j: (i, 0)),
        scratch_shapes=[pltpu.VMEM((tm, d), BF16), pltpu.VMEM((tm, d), F32)],
        compiler_params=_cparams("parallel", "arbitrary"),
        name="ffn",
    )(x, pre_w, w_gate, w_up, w_down, post_w)


def _swap_halves(w):
    half = w.shape[-1] // 2
    return jnp.concatenate([w[..., half:], w[..., :half]], axis=-1)


def _prep_w_in(w_in, b_gate, d_model):
    off_ssm = OFF_KPE + QK_ROPE_DIM
    off_gate = w_in.shape[1] - 2 * d_model
    kpe = w_in[:, OFF_KPE:off_ssm]
    pad = jnp.zeros((w_in.shape[0], SMALL_W - OFF_KPE - 2 * QK_ROPE_DIM), w_in.dtype)
    w = jnp.concatenate([w_in[:, :off_ssm], _swap_halves(kpe), pad, w_in[:, off_ssm:]], axis=1)
    gate_off = SMALL_W + (off_gate - off_ssm)
    bias = jnp.concatenate([jnp.zeros((gate_off,), F32), b_gate]).reshape(1, -1)
    return w.astype(BF16), bias, gate_off


def _prep_w_uq(w_uq):
    r = w_uq.shape[0]
    w = w_uq.reshape(r, MLA_HEADS, QK_NOPE_DIM + QK_ROPE_DIM)
    pe = w[..., QK_NOPE_DIM:]
    w = jnp.concatenate([w, _swap_halves(pe)], axis=-1)
    return w.reshape(r, MLA_HEADS * HEAD_PAD).astype(BF16)


def _prep_w_ukv(w_ukv):
    r = w_ukv.shape[0]
    w = w_ukv.reshape(r, MLA_HEADS, QK_NOPE_DIM + V_HEAD_DIM)
    wk = w[..., :QK_NOPE_DIM].reshape(r, MLA_HEADS * QK_NOPE_DIM)
    wv = w[..., QK_NOPE_DIM:].reshape(r, MLA_HEADS * V_HEAD_DIM)
    return wk.astype(BF16), wv.astype(BF16)


def kernel(x, positions, pre_mix_norm, w_in, b_gate, q_norm, kv_norm, w_uq, w_ukv, w_o_mla,
           ssm_a_re, ssm_a_im, ssm_log_dt, ssm_b_re, ssm_b_im, ssm_c_re, ssm_c_im, ssm_d,
           w_glu, b_glu, w_out, post_mix_norm, pre_ffn_norm, w_ffn_gate, w_ffn_up,
           w_ffn_down, post_ffn_norm):
    batch, seq, d = x.shape
    depth = w_in.shape[0]
    t = batch * seq
    groups = ssm_a_re.shape[1]
    ssm_w = groups * SSM_GROUP
    chunks = seq // SSM_CHUNK
    assert seq % max(CHUNK, SSM_CHUNK) == 0 and batch % 8 == 0

    tab = _rope_table(positions)
    xf = x.reshape(t, d)
    row = lambda v: v.reshape(1, -1)
    for l in range(depth):
        w_in_p, bias, gate_off = _prep_w_in(w_in[l], b_gate[l], d)
        proj = _in_proj(xf, row(pre_mix_norm[l]), w_in_p, bias, gate_off=gate_off)

        wk, wv = _prep_w_ukv(w_ukv[l])
        q, k, v = _qkv_up(proj, tab, row(q_norm[l]), row(kv_norm[l]), _prep_w_uq(w_uq[l]),
                          wk, wv, batch=batch, seq=seq)
        o = _attention(q, k, v).reshape(t, MLA_HEADS * V_HEAD_DIM)

        u = proj[:, SMALL_W:SMALL_W + ssm_w].reshape(batch, chunks, SSM_CHUNK, groups, SSM_GROUP)
        u_t = u.transpose(3, 1, 0, 2, 4).reshape(groups, chunks * batch, SSM_ROW)
        weights = _ssm_weights(ssm_a_re[l], ssm_a_im[l], ssm_log_dt[l], ssm_b_re[l], ssm_b_im[l],
                               ssm_c_re[l], ssm_c_im[l], ssm_d[l])
        z_t = _ssm(u_t, weights, batch=batch)
        z = z_t.reshape(groups, chunks, batch, SSM_CHUNK, SSM_GROUP).transpose(2, 1, 3, 0, 4)
        z = z.reshape(t, ssm_w)

        merged = _merge(o, z, proj, w_o_mla[l].astype(BF16), w_glu[l].astype(BF16),
                        row(b_glu[l]), gate_off=gate_off)
        xf = _out_proj(merged, xf, w_out[l].astype(BF16), row(post_mix_norm[l]))
        xf = _ffn(xf, row(pre_ffn_norm[l]), w_ffn_gate[l].astype(BF16), w_ffn_up[l].astype(BF16),
                  w_ffn_down[l].astype(BF16), row(post_ffn_norm[l]))
    return xf.reshape(batch, seq, d)
```

```python
import functools
import math

import jax
import jax.numpy as jnp
from jax import lax
from jax.experimental import pallas as pl
from jax.experimental.pallas import tpu as pltpu

F32 = jnp.float32
BF16 = jnp.bfloat16

CHUNK = 64
MLA_HEADS = 16
QK_NOPE_DIM = 128
QK_ROPE_DIM = 64
V_HEAD_DIM = 128
Q_LORA_RANK = 512
KV_LORA_RANK = 256
ROPE_THETA = 10000.0
SSM_GROUP = 16
SSM_STATE = 64
EPS = 1e-6

LANES = 128
HEAD_PAD = 256
SSM_CHUNK = 16
GROUPS_PER_BLOCK = LANES // SSM_GROUP
VMEM_LIMIT = 56 * 1024 * 1024
NEG_BIG = -1e30

SMALL_W = 1024
OFF_KPE = Q_LORA_RANK + KV_LORA_RANK


def _cparams(*sem):
    return pltpu.CompilerParams(dimension_semantics=sem, vmem_limit_bytes=VMEM_LIMIT)


def _rms(x, w):
    return x * lax.rsqrt(jnp.mean(x * x, axis=-1, keepdims=True) + EPS) * w


def _rope_table_kernel(pos_ref, freq_ref, tab_ref):
    ang = pos_ref[...].astype(F32) * freq_ref[...]
    lane = lax.broadcasted_iota(jnp.int32, ang.shape, 1)
    c = jnp.cos(ang)
    s = jnp.sin(ang)
    tab_ref[...] = jnp.where(lane < 64, c, jnp.where(lane < 96, -s, s))


def _rope_table(positions):
    t = positions.size
    tm = min(t, 2048)
    half = QK_ROPE_DIM // 2
    inv_freq = ROPE_THETA ** (-jnp.arange(0, QK_ROPE_DIM, 2, dtype=F32) / QK_ROPE_DIM)
    freq = jnp.tile(inv_freq, LANES // half).reshape(1, LANES)
    return pl.pallas_call(
        _rope_table_kernel,
        out_shape=jax.ShapeDtypeStruct((t, LANES), F32),
        grid=(t // tm,),
        in_specs=[pl.BlockSpec((tm, 1), lambda i: (i, 0)),
                  pl.BlockSpec((1, LANES), lambda i: (0, 0))],
        out_specs=pl.BlockSpec((tm, LANES), lambda i: (i, 0)),
        compiler_params=_cparams("parallel"),
        name="rope_table",
    )(positions.reshape(t, 1), freq)


def _in_proj_kernel(x_ref, nw_ref, w_ref, b_ref, p_ref, u_ref, h_ref):
    j = pl.program_id(1)

    @pl.when(j == 0)
    def _():
        h_ref[...] = _rms(x_ref[...], nw_ref[...]).astype(BF16)

    acc = jnp.dot(h_ref[...], w_ref[...], preferred_element_type=F32)

    @pl.when(j == 0)
    def _():
        p_ref[...] = acc.astype(BF16)

    @pl.when(j == 1)
    def _():
        u = acc.astype(BF16)
        for g in range(u_ref.shape[0]):
            for c in range(u_ref.shape[1]):
                u_ref[g, c] = u[c * SSM_CHUNK:(c + 1) * SSM_CHUNK, g * LANES:(g + 1) * LANES]

    @pl.when(j >= 2)
    def _():
        p_ref[...] = jax.nn.sigmoid(acc + b_ref[...]).astype(BF16)


def _in_proj(x, norm_w, w, bias, *, batch, seq):
    t, d = x.shape
    n = w.shape[1]
    tn = SMALL_W
    tm = min(seq, 1024)
    per_b = seq // tm
    cpt = tm // SSM_CHUNK
    nb = tn // LANES
    return pl.pallas_call(
        _in_proj_kernel,
        out_shape=(jax.ShapeDtypeStruct((t, n - tn), BF16),
                   jax.ShapeDtypeStruct((nb, seq // SSM_CHUNK, batch * SSM_CHUNK, LANES), BF16)),
        grid=(t // tm, n // tn),
        in_specs=[pl.BlockSpec((tm, d), lambda i, j: (i, 0)),
                  pl.BlockSpec((1, d), lambda i, j: (0, 0)),
                  pl.BlockSpec((d, tn), lambda i, j: (0, j)),
                  pl.BlockSpec((1, tn), lambda i, j: (0, j))],
        out_specs=(pl.BlockSpec((tm, tn), lambda i, j: (i, jnp.maximum(j - 1, 0))),
                   pl.BlockSpec((nb, cpt, SSM_CHUNK, LANES),
                                lambda i, j: (0, i % per_b, i // per_b, 0))),
        scratch_shapes=[pltpu.VMEM((tm, d), BF16)],
        compiler_params=_cparams("parallel", "arbitrary"),
        name="in_proj",
    )(x, norm_w, w, bias)


def _qkv_up_kernel(p_ref, tab_ref, qn_ref, kvn_ref, wq_ref, wk_ref, wv_ref,
                   q_ref, k_ref, v_ref, *, scale):
    tab = tab_ref[...]
    tabs = tab * scale
    cq = _rms(p_ref[:, 0:Q_LORA_RANK].astype(F32), qn_ref[...]).astype(BF16)
    q_all = jnp.dot(cq, wq_ref[...], preferred_element_type=F32)
    for h in range(MLA_HEADS):
        base = h * HEAD_PAD
        q_ref[h, :, 0:QK_NOPE_DIM] = (q_all[:, base:base + QK_NOPE_DIM] * scale).astype(BF16)
        q_ref[h, :, QK_NOPE_DIM:HEAD_PAD] = (
            q_all[:, base + QK_NOPE_DIM:base + HEAD_PAD] * tabs).astype(BF16)

    ckv = _rms(p_ref[:, Q_LORA_RANK:OFF_KPE].astype(F32), kvn_ref[...]).astype(BF16)
    k_all = jnp.dot(ckv, wk_ref[...], preferred_element_type=F32)
    v_all = jnp.dot(ckv, wv_ref[...], preferred_element_type=F32)
    kt = p_ref[:, OFF_KPE:OFF_KPE + LANES].astype(F32) * tab
    kr = (kt + pltpu.roll(kt, LANES // 2, 1)).astype(BF16)
    ones = jnp.ones((kr.shape[0], LANES), BF16)
    for h in range(MLA_HEADS):
        k_ref[h, :, 0:QK_NOPE_DIM] = k_all[:, h * QK_NOPE_DIM:(h + 1) * QK_NOPE_DIM].astype(BF16)
        k_ref[h, :, QK_NOPE_DIM:HEAD_PAD] = kr
        v_ref[h, :, 0:V_HEAD_DIM] = v_all[:, h * V_HEAD_DIM:(h + 1) * V_HEAD_DIM].astype(BF16)
        v_ref[h, :, V_HEAD_DIM:HEAD_PAD] = ones


def _qkv_up(proj, tab, q_norm, kv_norm, wq, wk, wv, *, batch, seq):
    tm = min(seq, 512)
    nb = seq // tm
    hshape = jax.ShapeDtypeStruct((batch, MLA_HEADS, seq, HEAD_PAD), BF16)
    hspec = pl.BlockSpec((None, MLA_HEADS, tm, HEAD_PAD), lambda b, i: (b, 0, i, 0))
    scale = (QK_NOPE_DIM + QK_ROPE_DIM) ** -0.5 * math.log2(math.e)
    const = lambda b, i: (0, 0)
    return pl.pallas_call(
        functools.partial(_qkv_up_kernel, scale=scale),
        out_shape=(hshape, hshape, hshape),
        grid=(batch, nb),
        in_specs=[pl.BlockSpec((tm, SMALL_W), lambda b, i: (b * nb + i, 0)),
                  pl.BlockSpec((tm, LANES), lambda b, i: (b * nb + i, 0)),
                  pl.BlockSpec((1, Q_LORA_RANK), const),
                  pl.BlockSpec((1, KV_LORA_RANK), const),
                  pl.BlockSpec(wq.shape, const),
                  pl.BlockSpec(wk.shape, const),
                  pl.BlockSpec(wv.shape, const)],
        out_specs=(hspec, hspec, hspec),
        compiler_params=_cparams("parallel", "parallel"),
        name="qkv_up",
    )(proj, tab, q_norm, kv_norm, wq, wk, wv)


def _attn_kernel(q_ref, k_ref, v_ref, o_ref, *, tq):
    seq = q_ref.shape[0]
    per = tq // LANES
    row_chunk = lax.broadcasted_iota(jnp.int32, (tq, LANES), 0) // CHUNK
    lane = lax.broadcasted_iota(jnp.int32, (tq, LANES), 1)
    masks = [((c * LANES + lane) // CHUNK) <= row_chunk for c in range(per)]
    for i in range(seq // tq):
        kv = (i + 1) * tq
        q = q_ref[i * tq:(i + 1) * tq, :]
        s = lax.dot_general(q, k_ref[0:kv, :], (((1,), (1,)), ((), ())),
                            preferred_element_type=F32)
        pieces = []
        for c in range(kv // LANES):
            piece = s[:, c * LANES:(c + 1) * LANES]
            if c >= i * per:
                piece = jnp.where(masks[c - i * per], piece, NEG_BIG)
            pieces.append(piece)
        mx = pieces[0]
        for piece in pieces[1:]:
            mx = jnp.maximum(mx, piece)
        m = jnp.broadcast_to(jnp.max(mx, axis=1, keepdims=True), (tq, LANES))
        p = jnp.concatenate([jnp.exp2(piece - m) for piece in pieces], axis=1).astype(BF16)
        acc = jnp.dot(p, v_ref[0:kv, :], preferred_element_type=F32)
        o_ref[i * tq:(i + 1) * tq, :] = (
            acc[:, 0:V_HEAD_DIM] / acc[:, V_HEAD_DIM:HEAD_PAD]).astype(o_ref.dtype)


def _attention(q, k, v):
    batch, heads, seq, _ = q.shape
    tq = min(seq, 512)
    spec = pl.BlockSpec((None, None, seq, HEAD_PAD), lambda b, h: (b, h, 0, 0))
    return pl.pallas_call(
        functools.partial(_attn_kernel, tq=tq),
        out_shape=jax.ShapeDtypeStruct((batch, seq, heads * V_HEAD_DIM), BF16),
        grid=(batch, heads),
        in_specs=[spec, spec, spec],
        out_specs=pl.BlockSpec((None, seq, V_HEAD_DIM), lambda b, h: (b, 0, h)),
        compiler_params=_cparams("parallel", "parallel"),
        name="attn",
    )(q, k, v)


def _ssm_weights(a_re, a_im, log_dt, b_re, b_im, c_re, c_im, d_skip):
    hi = lax.Precision.HIGHEST
    L = SSM_CHUNK
    g, n = a_re.shape
    p = SSM_GROUP
    gpb = GROUPS_PER_BLOCK
    nb = g // gpb
    dt = jnp.exp(log_dt)[:, None]
    mag = jnp.exp(a_re * dt)
    abar_re = mag * jnp.cos(a_im * dt)
    abar_im = mag * jnp.sin(a_im * dt)
    den = a_re * a_re + a_im * a_im
    nr = abar_re - 1.0
    f_re = (nr * a_re + abar_im * a_im) / den
    f_im = (abar_im * a_re - nr * a_im) / den
    bb_re = f_re[..., None] * b_re - f_im[..., None] * b_im
    bb_im = f_re[..., None] * b_im + f_im[..., None] * b_re
    kk = jnp.arange(L + 1, dtype=F32)[:, None, None]
    pmag = jnp.exp(kk * (a_re * dt))
    pw_re = pmag * jnp.cos(kk * (a_im * dt))
    pw_im = pmag * jnp.sin(kk * (a_im * dt))
    ca_re = c_re[None] * pw_re[:, :, None, :] - c_im[None] * pw_im[:, :, None, :]
    ca_im = c_re[None] * pw_im[:, :, None, :] + c_im[None] * pw_re[:, :, None, :]
    kern = (jnp.einsum('tgpn,gnq->tgpq', ca_re[:L], bb_re, precision=hi)
            - jnp.einsum('tgpn,gnq->tgpq', ca_im[:L], bb_im, precision=hi))
    kern = jnp.concatenate([kern, jnp.zeros((1, g, p, p), F32)], axis=0)
    ii = jnp.arange(L)
    lag = ii[None, :] - ii[:, None]
    lag = jnp.where(lag >= 0, lag, L)
    eye = jnp.eye(gpb, dtype=F32)

    toep = kern[lag].transpose(2, 0, 4, 1, 3)
    toep = toep.reshape(nb, gpb, L, p, L, p).transpose(0, 2, 1, 3, 4, 5)
    toep = toep[:, :, :, :, :, None, :] * eye[None, None, :, None, None, :, None]
    toep = toep.reshape(nb, L * LANES, L * LANES)

    rev_re = pw_re[L - 1 - ii]
    rev_im = pw_im[L - 1 - ii]

    def in_block(w):
        w = w.transpose(1, 0, 3, 2).reshape(nb, gpb, L, p, n).transpose(0, 2, 1, 3, 4)
        w = w[:, :, :, :, None, :] * eye[None, None, :, None, :, None]
        return w.reshape(nb, L * LANES, gpb * n)

    w_re = rev_re[:, :, :, None] * bb_re[None] - rev_im[:, :, :, None] * bb_im[None]
    w_im = rev_re[:, :, :, None] * bb_im[None] + rev_im[:, :, :, None] * bb_re[None]
    w_in = jnp.concatenate([in_block(w_re), in_block(w_im)], axis=-1)

    def out_block(z):
        z = z.transpose(1, 3, 0, 2).reshape(nb, gpb, n, L, p)
        z = z[:, :, :, :, None, :] * eye[None, :, None, None, :, None]
        return z.reshape(nb, gpb * n, L * LANES)

    z_out = jnp.concatenate([out_block(ca_re[1:]), out_block(-ca_im[1:])], axis=1)
    a_chunk = jnp.concatenate([pw_re[L].reshape(nb, 1, gpb * n), pw_im[L].reshape(nb, 1, gpb * n)],
                              axis=-1)
    d_row = jnp.tile(d_skip.reshape(nb, 1, LANES), (1, 1, L))
    return toep.astype(BF16), w_in.astype(BF16), z_out.astype(BF16), a_chunk, d_row


def _ssm_in_kernel(u_ref, w_ref, v_ref):
    v_ref[...] = jnp.dot(u_ref[...], w_ref[...], preferred_element_type=F32)


def _ssm_scan_kernel(v_ref, a_ref, h_ref, *, batch, chunks):
    half = v_ref.shape[-1] // 2
    ar = jnp.broadcast_to(a_ref[:, 0:half], (batch, half))
    ai = jnp.broadcast_to(a_ref[:, half:], (batch, half))

    def step(c, carry):
        hr, hi = carry
        rows = pl.ds(pl.multiple_of(c * batch, batch), batch)
        h_ref[rows, 0:half] = hr
        h_ref[rows, half:] = hi
        return (ar * hr - ai * hi + v_ref[rows, 0:half],
                ar * hi + ai * hr + v_ref[rows, half:])

    zero = jnp.zeros((batch, half), F32)
    lax.fori_loop(0, chunks, step, (zero, zero))


def _ssm_out_kernel(u_ref, h_ref, t_ref, z_ref, d_ref, o_ref):
    u = u_ref[...]
    y = (jnp.dot(u, t_ref[...], preferred_element_type=F32)
         + jnp.dot(h_ref[...].astype(BF16), z_ref[...], preferred_element_type=F32)
         + u.astype(F32) * d_ref[...])
    o_ref[...] = jax.nn.gelu(y).astype(o_ref.dtype)


def _ssm(u, weights, *, batch):
    toep, w_in, z_out, a_chunk, d_row = weights
    nb, m, width = u.shape
    states = w_in.shape[-1]
    tr = min(m, 512)
    chunks = m // batch
    gconst = lambda g, r: (g, 0, 0)
    grow = lambda g, r: (g, r, 0)

    v = pl.pallas_call(
        _ssm_in_kernel,
        out_shape=jax.ShapeDtypeStruct((nb, m, states), F32),
        grid=(nb, m // tr),
        in_specs=[pl.BlockSpec((None, tr, width), grow),
                  pl.BlockSpec((None,) + w_in.shape[1:], gconst)],
        out_specs=pl.BlockSpec((None, tr, states), grow),
        compiler_params=_cparams("parallel", "parallel"),
        name="ssm_in",
    )(u, w_in)

    h = pl.pallas_call(
        functools.partial(_ssm_scan_kernel, batch=batch, chunks=chunks),
        out_shape=jax.ShapeDtypeStruct((nb, m, states), F32),
        grid=(nb,),
        in_specs=[pl.BlockSpec((None, m, states), lambda g: (g, 0, 0)),
                  pl.BlockSpec((None, 1, states), lambda g: (g, 0, 0))],
        out_specs=pl.BlockSpec((None, m, states), lambda g: (g, 0, 0)),
        compiler_params=_cparams("parallel"),
        name="ssm_scan",
    )(v, a_chunk)

    return pl.pallas_call(
        _ssm_out_kernel,
        out_shape=jax.ShapeDtypeStruct(u.shape, BF16),
        grid=(nb, m // tr),
        in_specs=[pl.BlockSpec((None, tr, width), grow),
                  pl.BlockSpec((None, tr, states), grow),
                  pl.BlockSpec((None,) + toep.shape[1:], gconst),
                  pl.BlockSpec((None,) + z_out.shape[1:], gconst),
                  pl.BlockSpec((None,) + d_row.shape[1:], gconst)],
        out_specs=pl.BlockSpec((None, tr, width), grow),
        compiler_params=_cparams("parallel", "parallel"),
        name="ssm_out",
    )(u, h, toep, z_out, d_row)


def _merge_kernel(o_ref, z_ref, ga_ref, gb_ref, wo_ref, w1_ref, w2_ref, b1_ref, b2_ref, m_ref):
    a = jnp.dot(o_ref[...], wo_ref[...], preferred_element_type=F32)
    tm = o_ref.shape[0]
    z = jnp.concatenate([z_ref[g].reshape(tm, LANES) for g in range(z_ref.shape[0])], axis=1)
    z1 = jnp.dot(z, w1_ref[...], preferred_element_type=F32) + b1_ref[...]
    z2 = jnp.dot(z, w2_ref[...], preferred_element_type=F32) + b2_ref[...]
    s = z1 * jax.nn.sigmoid(z2)
    m_ref[...] = (ga_ref[...].astype(F32) * a + gb_ref[...].astype(F32) * s).astype(m_ref.dtype)


def _merge(o, z_cb, proj, w_o, w_glu, b_glu, *, seq):
    t, d = o.shape
    nb = z_cb.shape[0]
    zw = nb * LANES
    tm = min(seq, 1024)
    per_b = seq // tm
    tn = 512
    nc = d // tn
    ga0 = SMALL_W // tn
    return pl.pallas_call(
        _merge_kernel,
        out_shape=jax.ShapeDtypeStruct((t, d), BF16),
        grid=(t // tm, nc),
        in_specs=[pl.BlockSpec((tm, d), lambda i, j: (i, 0)),
                  pl.BlockSpec((nb, tm // SSM_CHUNK, SSM_CHUNK, LANES),
                               lambda i, j: (0, i % per_b, i // per_b, 0)),
                  pl.BlockSpec((tm, tn), lambda i, j: (i, ga0 + j)),
                  pl.BlockSpec((tm, tn), lambda i, j: (i, ga0 + nc + j)),
                  pl.BlockSpec((d, tn), lambda i, j: (0, j)),
                  pl.BlockSpec((zw, tn), lambda i, j: (0, j)),
                  pl.BlockSpec((zw, tn), lambda i, j: (0, nc + j)),
                  pl.BlockSpec((1, tn), lambda i, j: (0, j)),
                  pl.BlockSpec((1, tn), lambda i, j: (0, nc + j))],
        out_specs=pl.BlockSpec((tm, tn), lambda i, j: (i, j)),
        compiler_params=_cparams("parallel", "arbitrary"),
        name="merge",
    )(o, z_cb, proj, proj, w_o, w_glu, w_glu, b_glu, b_glu)


def _out_proj_kernel(m_ref, x_ref, w_ref, nw_ref, o_ref):
    mix = jnp.dot(m_ref[...], w_ref[...], preferred_element_type=F32)
    o_ref[...] = x_ref[...] + _rms(mix, nw_ref[...])


def _out_proj(merged, x, w_out, norm_w):
    t, d = x.shape
    tm = min(t, 512)
    return pl.pallas_call(
        _out_proj_kernel,
        out_shape=jax.ShapeDtypeStruct((t, d), F32),
        grid=(t // tm,),
        in_specs=[pl.BlockSpec((tm, d), lambda i: (i, 0)),
                  pl.BlockSpec((tm, d), lambda i: (i, 0)),
                  pl.BlockSpec((d, d), lambda i: (0, 0)),
                  pl.BlockSpec((1, d), lambda i: (0, 0))],
        out_specs=pl.BlockSpec((tm, d), lambda i: (i, 0)),
        compiler_params=_cparams("parallel"),
        name="out_proj",
    )(merged, x, w_out, norm_w)


def _ffn_kernel(x_ref, nw_ref, wg_ref, wu_ref, wd_ref, pw_ref, o_ref, h_ref, acc_ref):
    j = pl.program_id(1)

    @pl.when(j == 0)
    def _():
        h_ref[...] = _rms(x_ref[...], nw_ref[...]).astype(BF16)
        acc_ref[...] = jnp.zeros(acc_ref.shape, F32)

    h = h_ref[...]
    gate = jnp.dot(h, wg_ref[...], preferred_element_type=F32)
    up = jnp.dot(h, wu_ref[...], preferred_element_type=F32)
    act = (jax.nn.silu(gate) * up).astype(BF16)
    acc_ref[...] += jnp.dot(act, wd_ref[...], preferred_element_type=F32)

    @pl.when(j == pl.num_programs(1) - 1)
    def _():
        o_ref[...] = x_ref[...] + _rms(acc_ref[...], pw_ref[...])


def _ffn(x, pre_w, w_gate, w_up, w_down, post_w):
    t, d = x.shape
    f = w_gate.shape[1]
    tm = min(t, 512)
    tf = 512
    return pl.pallas_call(
        _ffn_kernel,
        out_shape=jax.ShapeDtypeStruct((t, d), F32),
        grid=(t // tm, f // tf),
        in_specs=[pl.BlockSpec((tm, d), lambda i, j: (i, 0)),
                  pl.BlockSpec((1, d), lambda i, j: (0, 0)),
                  pl.BlockSpec((d, tf), lambda i, j: (0, j)),
                  pl.BlockSpec((d, tf), lambda i, j: (0, j)),
                  pl.BlockSpec((tf, d), lambda i, j: (j, 0)),
                  pl.BlockSpec((1, d), lambda i, j: (0, 0))],
        out_specs=pl.BlockSpec((tm, d), lambda i, j: (i, 0)),
        scratch_shapes=[pltpu.VMEM((tm, d), BF16), pltpu.VMEM((tm, d), F32)],
        compiler_params=_cparams("parallel", "arbitrary"),
        name="ffn",
    )(x, pre_w, w_gate, w_up, w_down, post_w)


def _swap_halves(w):
    half = w.shape[-1] // 2
    return jnp.concatenate([w[..., half:], w[..., :half]], axis=-1)


def _prep_w_in(w_in, b_gate):
    off_ssm = OFF_KPE + QK_ROPE_DIM
    kpe = w_in[:, OFF_KPE:off_ssm]
    pad = jnp.zeros((w_in.shape[0], SMALL_W - OFF_KPE - 2 * QK_ROPE_DIM), w_in.dtype)
    w = jnp.concatenate([w_in[:, :off_ssm], _swap_halves(kpe), pad, w_in[:, off_ssm:]], axis=1)
    bias = jnp.concatenate([jnp.zeros((w.shape[1] - b_gate.shape[0],), F32), b_gate]).reshape(1, -1)
    return w.astype(BF16), bias


def _prep_w_uq(w_uq):
    r = w_uq.shape[0]
    w = w_uq.reshape(r, MLA_HEADS, QK_NOPE_DIM + QK_ROPE_DIM)
    pe = w[..., QK_NOPE_DIM:]
    w = jnp.concatenate([w, _swap_halves(pe)], axis=-1)
    return w.reshape(r, MLA_HEADS * HEAD_PAD).astype(BF16)


def _prep_w_ukv(w_ukv):
    r = w_ukv.shape[0]
    w = w_ukv.reshape(r, MLA_HEADS, QK_NOPE_DIM + V_HEAD_DIM)
    wk = w[..., :QK_NOPE_DIM].reshape(r, MLA_HEADS * QK_NOPE_DIM)
    wv = w[..., QK_NOPE_DIM:].reshape(r, MLA_HEADS * V_HEAD_DIM)
    return wk.astype(BF16), wv.astype(BF16)


def kernel(x, positions, pre_mix_norm, w_in, b_gate, q_norm, kv_norm, w_uq, w_ukv, w_o_mla,
           ssm_a_re, ssm_a_im, ssm_log_dt, ssm_b_re, ssm_b_im, ssm_c_re, ssm_c_im, ssm_d,
           w_glu, b_glu, w_out, post_mix_norm, pre_ffn_norm, w_ffn_gate, w_ffn_up,
           w_ffn_down, post_ffn_norm):
    batch, seq, d = x.shape
    depth = w_in.shape[0]
    t = batch * seq
    ssm_w = ssm_a_re.shape[1] * SSM_GROUP
    chunks = seq // SSM_CHUNK
    assert seq % max(CHUNK, SSM_CHUNK) == 0 and batch % 8 == 0
    assert ssm_w == SMALL_W and d % SMALL_W == 0

    tab = _rope_table(positions)
    xf = x.reshape(t, d)
    row = lambda v: v.reshape(1, -1)
    for l in range(depth):
        w_in_p, bias = _prep_w_in(w_in[l], b_gate[l])
        proj, u_cb = _in_proj(xf, row(pre_mix_norm[l]), w_in_p, bias, batch=batch, seq=seq)

        wk, wv = _prep_w_ukv(w_ukv[l])
        q, k, v = _qkv_up(proj, tab, row(q_norm[l]), row(kv_norm[l]), _prep_w_uq(w_uq[l]),
                          wk, wv, batch=batch, seq=seq)
        o = _attention(q, k, v).reshape(t, MLA_HEADS * V_HEAD_DIM)

        weights = _ssm_weights(ssm_a_re[l], ssm_a_im[l], ssm_log_dt[l], ssm_b_re[l], ssm_b_im[l],
                               ssm_c_re[l], ssm_c_im[l], ssm_d[l])
        nb = u_cb.shape[0]
        z_cb = _ssm(u_cb.reshape(nb, chunks * batch, SSM_CHUNK * LANES), weights, batch=batch)
        z_cb = z_cb.reshape(u_cb.shape)

        merged = _merge(o, z_cb, proj, w_o_mla[l].astype(BF16), w_glu[l].astype(BF16),
                        row(b_glu[l]), seq=seq)
        xf = _out_proj(merged, xf, w_out[l].astype(BF16), row(post_mix_norm[l]))
        xf = _ffn(xf, row(pre_ffn_norm[l]), w_ffn_gate[l].astype(BF16), w_ffn_up[l].astype(BF16),
                  w_ffn_down[l].astype(BF16), row(post_ffn_norm[l]))
    return xf.reshape(batch, seq, d)
```

```python
import functools
import math

import jax
import jax.numpy as jnp
from jax import lax
from jax.experimental import pallas as pl
from jax.experimental.pallas import tpu as pltpu

F32 = jnp.float32
BF16 = jnp.bfloat16

CHUNK = 64
MLA_HEADS = 16
QK_NOPE_DIM = 128
QK_ROPE_DIM = 64
V_HEAD_DIM = 128
Q_LORA_RANK = 512
KV_LORA_RANK = 256
ROPE_THETA = 10000.0
SSM_GROUP = 16
SSM_STATE = 64
EPS = 1e-6

LANES = 128
HEAD_PAD = 256
SSM_CHUNK = 16
GROUPS_PER_BLOCK = LANES // SSM_GROUP
VMEM_LIMIT = 56 * 1024 * 1024
NEG_BIG = -1e30

SMALL_W = 1024
OFF_KPE = Q_LORA_RANK + KV_LORA_RANK


def _cparams(*sem):
    return pltpu.CompilerParams(dimension_semantics=sem, vmem_limit_bytes=VMEM_LIMIT)


def _rms(x, w):
    return x * lax.rsqrt(jnp.mean(x * x, axis=-1, keepdims=True) + EPS) * w


def _rope_table_kernel(pos_ref, freq_ref, tab_ref):
    ang = pos_ref[...].astype(F32) * freq_ref[...]
    lane = lax.broadcasted_iota(jnp.int32, ang.shape, 1)
    c = jnp.cos(ang)
    s = jnp.sin(ang)
    tab_ref[...] = jnp.where(lane < 64, c, jnp.where(lane < 96, -s, s))


def _rope_table(positions):
    t = positions.size
    tm = min(t, 2048)
    half = QK_ROPE_DIM // 2
    inv_freq = ROPE_THETA ** (-jnp.arange(0, QK_ROPE_DIM, 2, dtype=F32) / QK_ROPE_DIM)
    freq = jnp.tile(inv_freq, LANES // half).reshape(1, LANES)
    return pl.pallas_call(
        _rope_table_kernel,
        out_shape=jax.ShapeDtypeStruct((t, LANES), F32),
        grid=(t // tm,),
        in_specs=[pl.BlockSpec((tm, 1), lambda i: (i, 0)),
                  pl.BlockSpec((1, LANES), lambda i: (0, 0))],
        out_specs=pl.BlockSpec((tm, LANES), lambda i: (i, 0)),
        compiler_params=_cparams("parallel"),
        name="rope_table",
    )(positions.reshape(t, 1), freq)


def _in_proj_kernel(x_ref, nw_ref, w_ref, b_ref, p_ref, u_ref, h_ref):
    j = pl.program_id(1)

    @pl.when(j == 0)
    def _():
        h_ref[...] = _rms(x_ref[...], nw_ref[...]).astype(BF16)

    acc = jnp.dot(h_ref[...], w_ref[...], preferred_element_type=F32)

    gated = jax.nn.sigmoid(acc + b_ref[...])
    p_ref[...] = jnp.where(j >= 2, gated, acc).astype(BF16)

    @pl.when(j == 0)
    def _():
        u = acc.astype(BF16)
        for g in range(u_ref.shape[0]):
            for c in range(u_ref.shape[1]):
                u_ref[g, c] = u[c * SSM_CHUNK:(c + 1) * SSM_CHUNK, g * LANES:(g + 1) * LANES]


def _in_proj(x, norm_w, w, bias, *, batch, seq):
    t, d = x.shape
    n = w.shape[1]
    tn = SMALL_W
    tm = min(seq, 1024)
    per_b = seq // tm
    cpt = tm // SSM_CHUNK
    nb = tn // LANES
    return pl.pallas_call(
        _in_proj_kernel,
        out_shape=(jax.ShapeDtypeStruct((t, n - tn), BF16),
                   jax.ShapeDtypeStruct((nb, seq // SSM_CHUNK, batch * SSM_CHUNK, LANES), BF16)),
        grid=(t // tm, n // tn),
        in_specs=[pl.BlockSpec((tm, d), lambda i, j: (i, 0)),
                  pl.BlockSpec((1, d), lambda i, j: (0, 0)),
                  pl.BlockSpec((d, tn), lambda i, j: (0, j)),
                  pl.BlockSpec((1, tn), lambda i, j: (0, j))],
        out_specs=(pl.BlockSpec((tm, tn), lambda i, j: (i, jnp.maximum(j - 1, 0))),
                   pl.BlockSpec((nb, cpt, SSM_CHUNK, LANES),
                                lambda i, j: (0, i % per_b, i // per_b, 0))),
        scratch_shapes=[pltpu.VMEM((tm, d), BF16)],
        compiler_params=_cparams("parallel", "arbitrary"),
        name="in_proj",
    )(x, norm_w, w, bias)


def _qkv_up_kernel(p_ref, tab_ref, qn_ref, kvn_ref, wq_ref, wk_ref, wv_ref,
                   q_ref, k_ref, v_ref, *, scale):
    tab = tab_ref[...]
    tabs = tab * scale
    cq = _rms(p_ref[:, 0:Q_LORA_RANK].astype(F32), qn_ref[...]).astype(BF16)
    q_all = jnp.dot(cq, wq_ref[...], preferred_element_type=F32)
    for h in range(MLA_HEADS):
        base = h * HEAD_PAD
        q_ref[h, :, 0:QK_NOPE_DIM] = (q_all[:, base:base + QK_NOPE_DIM] * scale).astype(BF16)
        q_ref[h, :, QK_NOPE_DIM:HEAD_PAD] = (
            q_all[:, base + QK_NOPE_DIM:base + HEAD_PAD] * tabs).astype(BF16)

    ckv = _rms(p_ref[:, Q_LORA_RANK:OFF_KPE].astype(F32), kvn_ref[...]).astype(BF16)
    k_all = jnp.dot(ckv, wk_ref[...], preferred_element_type=F32)
    v_all = jnp.dot(ckv, wv_ref[...], preferred_element_type=F32)
    kt = p_ref[:, OFF_KPE:OFF_KPE + LANES].astype(F32) * tab
    kr = (kt + pltpu.roll(kt, LANES // 2, 1)).astype(BF16)
    ones = jnp.ones((kr.shape[0], LANES), BF16)
    for h in range(MLA_HEADS):
        k_ref[h, :, 0:QK_NOPE_DIM] = k_all[:, h * QK_NOPE_DIM:(h + 1) * QK_NOPE_DIM].astype(BF16)
        k_ref[h, :, QK_NOPE_DIM:HEAD_PAD] = kr
        v_ref[h, :, 0:V_HEAD_DIM] = v_all[:, h * V_HEAD_DIM:(h + 1) * V_HEAD_DIM].astype(BF16)
        v_ref[h, :, V_HEAD_DIM:HEAD_PAD] = ones


def _qkv_up(proj, tab, q_norm, kv_norm, wq, wk, wv, *, batch, seq):
    tm = min(seq, 512)
    nb = seq // tm
    hshape = jax.ShapeDtypeStruct((batch, MLA_HEADS, seq, HEAD_PAD), BF16)
    hspec = pl.BlockSpec((None, MLA_HEADS, tm, HEAD_PAD), lambda b, i: (b, 0, i, 0))
    scale = (QK_NOPE_DIM + QK_ROPE_DIM) ** -0.5 * math.log2(math.e)
    const = lambda b, i: (0, 0)
    return pl.pallas_call(
        functools.partial(_qkv_up_kernel, scale=scale),
        out_shape=(hshape, hshape, hshape),
        grid=(batch, nb),
        in_specs=[pl.BlockSpec((tm, SMALL_W), lambda b, i: (b * nb + i, 0)),
                  pl.BlockSpec((tm, LANES), lambda b, i: (b * nb + i, 0)),
                  pl.BlockSpec((1, Q_LORA_RANK), const),
                  pl.BlockSpec((1, KV_LORA_RANK), const),
                  pl.BlockSpec(wq.shape, const),
                  pl.BlockSpec(wk.shape, const),
                  pl.BlockSpec(wv.shape, const)],
        out_specs=(hspec, hspec, hspec),
        compiler_params=_cparams("parallel", "parallel"),
        name="qkv_up",
    )(proj, tab, q_norm, kv_norm, wq, wk, wv)


def _attn_kernel(q_ref, k_ref, v_ref, o_ref, *, tq):
    seq = q_ref.shape[0]
    per = tq // LANES
    row_chunk = lax.broadcasted_iota(jnp.int32, (tq, LANES), 0) // CHUNK
    lane = lax.broadcasted_iota(jnp.int32, (tq, LANES), 1)
    masks = [((c * LANES + lane) // CHUNK) <= row_chunk for c in range(per)]
    for i in range(seq // tq):
        kv = (i + 1) * tq
        q = q_ref[i * tq:(i + 1) * tq, :]
        s = lax.dot_general(q, k_ref[0:kv, :], (((1,), (1,)), ((), ())),
                            preferred_element_type=F32)
        pieces = []
        for c in range(kv // LANES):
            piece = s[:, c * LANES:(c + 1) * LANES]
            if c >= i * per:
                piece = jnp.where(masks[c - i * per], piece, NEG_BIG)
            pieces.append(piece)
        mx = pieces[0]
        for piece in pieces[1:]:
            mx = jnp.maximum(mx, piece)
        m = jnp.broadcast_to(jnp.max(mx, axis=1, keepdims=True), (tq, LANES))
        p = jnp.concatenate([jnp.exp2(piece - m) for piece in pieces], axis=1).astype(BF16)
        acc = jnp.dot(p, v_ref[0:kv, :], preferred_element_type=F32)
        o_ref[i * tq:(i + 1) * tq, :] = (
            acc[:, 0:V_HEAD_DIM] / acc[:, V_HEAD_DIM:HEAD_PAD]).astype(o_ref.dtype)


def _attention(q, k, v):
    batch, heads, seq, _ = q.shape
    tq = min(seq, 512)
    spec = pl.BlockSpec((None, None, seq, HEAD_PAD), lambda b, h: (b, h, 0, 0))
    return pl.pallas_call(
        functools.partial(_attn_kernel, tq=tq),
        out_shape=jax.ShapeDtypeStruct((batch, seq, heads * V_HEAD_DIM), BF16),
        grid=(batch, heads),
        in_specs=[spec, spec, spec],
        out_specs=pl.BlockSpec((None, seq, V_HEAD_DIM), lambda b, h: (b, 0, h)),
        compiler_params=_cparams("parallel", "parallel"),
        name="attn",
    )(q, k, v)


def _ssm_weights(a_re, a_im, log_dt, b_re, b_im, c_re, c_im, d_skip):
    hi = lax.Precision.HIGHEST
    L = SSM_CHUNK
    g, n = a_re.shape
    p = SSM_GROUP
    gpb = GROUPS_PER_BLOCK
    nb = g // gpb
    dt = jnp.exp(log_dt)[:, None]
    mag = jnp.exp(a_re * dt)
    abar_re = mag * jnp.cos(a_im * dt)
    abar_im = mag * jnp.sin(a_im * dt)
    den = a_re * a_re + a_im * a_im
    nr = abar_re - 1.0
    f_re = (nr * a_re + abar_im * a_im) / den
    f_im = (abar_im * a_re - nr * a_im) / den
    bb_re = f_re[..., None] * b_re - f_im[..., None] * b_im
    bb_im = f_re[..., None] * b_im + f_im[..., None] * b_re
    kk = jnp.arange(L + 1, dtype=F32)[:, None, None]
    pmag = jnp.exp(kk * (a_re * dt))
    pw_re = pmag * jnp.cos(kk * (a_im * dt))
    pw_im = pmag * jnp.sin(kk * (a_im * dt))
    ca_re = c_re[None] * pw_re[:L, :, None, :] - c_im[None] * pw_im[:L, :, None, :]
    ca_im = c_re[None] * pw_im[:L, :, None, :] + c_im[None] * pw_re[:L, :, None, :]
    kern = (jnp.einsum('tgpn,gnq->tgpq', ca_re, bb_re, precision=hi)
            - jnp.einsum('tgpn,gnq->tgpq', ca_im, bb_im, precision=hi))
    eye = jnp.eye(gpb, dtype=F32)

    def block_diag(a):
        lead = a.shape[:-3]
        r, c = a.shape[-2:]
        a = a.reshape(lead + (nb, gpb, r, 1, c)) * eye[:, None, :, None]
        return a.reshape(lead + (nb, gpb * r, gpb * c))

    kern_bd = block_diag(kern.transpose(0, 1, 3, 2)).transpose(1, 0, 2, 3)
    bb_bd = (block_diag(bb_re.transpose(0, 2, 1)), block_diag(bb_im.transpose(0, 2, 1)))
    c_bd = (block_diag(c_re.transpose(0, 2, 1)), block_diag(c_im.transpose(0, 2, 1)))
    pw = jnp.concatenate([pw_re.reshape(L + 1, nb, gpb * n), pw_im.reshape(L + 1, nb, gpb * n)],
                         axis=-1).transpose(1, 0, 2)
    pw_col = jnp.concatenate([pw_re.reshape(L + 1, nb, gpb * n), pw_im.reshape(L + 1, nb, gpb * n)],
                             axis=0).transpose(1, 2, 0)
    d_row = jnp.tile(d_skip.reshape(nb, 1, LANES), (1, 1, L))
    return kern_bd.astype(BF16), bb_bd, c_bd, pw, pw_col, d_row


def _ssm_in_kernel(u_ref, bbr_ref, bbi_ref, pw_ref, v_ref, w_ref):
    @pl.when(pl.program_id(1) == 0)
    def _():
        half = bbr_ref.shape[-1]
        bbr = bbr_ref[...]
        bbi = bbi_ref[...]
        for i in range(SSM_CHUNK):
            k = SSM_CHUNK - 1 - i
            pr = pw_ref[k:k + 1, 0:half]
            pi = pw_ref[k:k + 1, half:]
            w_ref[i * LANES:(i + 1) * LANES, 0:half] = (pr * bbr - pi * bbi).astype(BF16)
            w_ref[i * LANES:(i + 1) * LANES, half:] = (pr * bbi + pi * bbr).astype(BF16)

    v_ref[...] = jnp.dot(u_ref[...], w_ref[...], preferred_element_type=F32)


def _ssm_scan_kernel(v_ref, a_ref, h_ref, *, batch, chunks):
    half = v_ref.shape[-1] // 2
    ar = jnp.broadcast_to(a_ref[:, 0:half], (batch, half))
    ai = jnp.broadcast_to(a_ref[:, half:], (batch, half))

    def step(c, carry):
        hr, hi = carry
        rows = pl.ds(pl.multiple_of(c * batch, batch), batch)
        h_ref[rows, 0:half] = hr
        h_ref[rows, half:] = hi
        return (ar * hr - ai * hi + v_ref[rows, 0:half],
                ar * hi + ai * hr + v_ref[rows, half:])

    zero = jnp.zeros((batch, half), F32)
    lax.fori_loop(0, chunks, step, (zero, zero))


def _ssm_out_kernel(u_ref, h_ref, kern_ref, cr_ref, ci_ref, pc_ref, d_ref, o_ref, t_ref, z_ref):
    @pl.when(pl.program_id(1) == 0)
    def _():
        L = SSM_CHUNK
        zero = jnp.zeros((LANES, LANES), BF16)
        for i in range(L):
            for j in range(L):
                t_ref[i * LANES:(i + 1) * LANES, j * LANES:(j + 1) * LANES] = (
                    kern_ref[j - i] if j >= i else zero)
        half = cr_ref.shape[0]
        cr = cr_ref[...]
        ci = ci_ref[...]
        for j in range(L):
            pr = pc_ref[:, j + 1:j + 2]
            pi = pc_ref[:, L + j + 2:L + j + 3]
            z_ref[0:half, j * LANES:(j + 1) * LANES] = (cr * pr - ci * pi).astype(BF16)
            z_ref[half:, j * LANES:(j + 1) * LANES] = (-(cr * pi + ci * pr)).astype(BF16)

    u = u_ref[...]
    y = (jnp.dot(u, t_ref[...], preferred_element_type=F32)
         + jnp.dot(h_ref[...].astype(BF16), z_ref[...], preferred_element_type=F32)
         + u.astype(F32) * d_ref[...])
    o_ref[...] = jax.nn.gelu(y).astype(o_ref.dtype)


def _ssm(u, weights, *, batch):
    kern_bd, (bb_r, bb_i), (c_r, c_i), pw, pw_col, d_row = weights
    nb, m, width = u.shape
    states = pw.shape[-1]
    tr = min(m, 512)
    chunks = m // batch
    grow = lambda g, r: (g, r, 0)

    def whole(a):
        return pl.BlockSpec((None,) + a.shape[1:], lambda g, r: (g,) + (0,) * (a.ndim - 1))

    v = pl.pallas_call(
        _ssm_in_kernel,
        out_shape=jax.ShapeDtypeStruct((nb, m, states), F32),
        grid=(nb, m // tr),
        in_specs=[pl.BlockSpec((None, tr, width), grow), whole(bb_r), whole(bb_i), whole(pw)],
        out_specs=pl.BlockSpec((None, tr, states), grow),
        scratch_shapes=[pltpu.VMEM((width, states), BF16)],
        compiler_params=_cparams("parallel", "arbitrary"),
        name="ssm_in",
    )(u, bb_r, bb_i, pw)

    h = pl.pallas_call(
        functools.partial(_ssm_scan_kernel, batch=batch, chunks=chunks),
        out_shape=jax.ShapeDtypeStruct((nb, m, states), F32),
        grid=(nb,),
        in_specs=[pl.BlockSpec((None, m, states), lambda g: (g, 0, 0)),
                  pl.BlockSpec((None, 1, states), lambda g: (g, 0, 0))],
        out_specs=pl.BlockSpec((None, m, states), lambda g: (g, 0, 0)),
        compiler_params=_cparams("parallel"),
        name="ssm_scan",
    )(v, pw[:, SSM_CHUNK:SSM_CHUNK + 1, :])

    return pl.pallas_call(
        _ssm_out_kernel,
        out_shape=jax.ShapeDtypeStruct(u.shape, BF16),
        grid=(nb, m // tr),
        in_specs=[pl.BlockSpec((None, tr, width), grow),
                  pl.BlockSpec((None, tr, states), grow),
                  whole(kern_bd), whole(c_r), whole(c_i), whole(pw_col), whole(d_row)],
        out_specs=pl.BlockSpec((None, tr, width), grow),
        scratch_shapes=[pltpu.VMEM((width, width), BF16), pltpu.VMEM((states, width), BF16)],
        compiler_params=_cparams("parallel", "arbitrary"),
        name="ssm_out",
    )(u, h, kern_bd, c_r, c_i, pw_col, d_row)


def _merge_kernel(o_ref, z_ref, ga_ref, gb_ref, wo_ref, w1_ref, w2_ref, b1_ref, b2_ref, m_ref):
    a = jnp.dot(o_ref[...], wo_ref[...], preferred_element_type=F32)
    tm = o_ref.shape[0]
    z = jnp.concatenate([z_ref[g].reshape(tm, LANES) for g in range(z_ref.shape[0])], axis=1)
    z1 = jnp.dot(z, w1_ref[...], preferred_element_type=F32) + b1_ref[...]
    z2 = jnp.dot(z, w2_ref[...], preferred_element_type=F32) + b2_ref[...]
    s = z1 * jax.nn.sigmoid(z2)
    m_ref[...] = (ga_ref[...].astype(F32) * a + gb_ref[...].astype(F32) * s).astype(m_ref.dtype)


def _merge(o, z_cb, proj, w_o, w_glu, b_glu, *, seq):
    t, d = o.shape
    nb = z_cb.shape[0]
    zw = nb * LANES
    tm = min(seq, 1024)
    per_b = seq // tm
    tn = 512
    nc = d // tn
    ga0 = SMALL_W // tn
    return pl.pallas_call(
        _merge_kernel,
        out_shape=jax.ShapeDtypeStruct((t, d), BF16),
        grid=(t // tm, nc),
        in_specs=[pl.BlockSpec((tm, d), lambda i, j: (i, 0)),
                  pl.BlockSpec((nb, tm // SSM_CHUNK, SSM_CHUNK, LANES),
                               lambda i, j: (0, i % per_b, i // per_b, 0)),
                  pl.BlockSpec((tm, tn), lambda i, j: (i, ga0 + j)),
                  pl.BlockSpec((tm, tn), lambda i, j: (i, ga0 + nc + j)),
                  pl.BlockSpec((d, tn), lambda i, j: (0, j)),
                  pl.BlockSpec((zw, tn), lambda i, j: (0, j)),
                  pl.BlockSpec((zw, tn), lambda i, j: (0, nc + j)),
                  pl.BlockSpec((1, tn), lambda i, j: (0, j)),
                  pl.BlockSpec((1, tn), lambda i, j: (0, nc + j))],
        out_specs=pl.BlockSpec((tm, tn), lambda i, j: (i, j)),
        compiler_params=_cparams("parallel", "arbitrary"),
        name="merge",
    )(o, z_cb, proj, proj, w_o, w_glu, w_glu, b_glu, b_glu)


def _out_proj_kernel(m_ref, x_ref, w_ref, nw_ref, o_ref):
    mix = jnp.dot(m_ref[...], w_ref[...], preferred_element_type=F32)
    o_ref[...] = x_ref[...] + _rms(mix, nw_ref[...])


def _out_proj(merged, x, w_out, norm_w):
    t, d = x.shape
    tm = min(t, 512)
    return pl.pallas_call(
        _out_proj_kernel,
        out_shape=jax.ShapeDtypeStruct((t, d), F32),
        grid=(t // tm,),
        in_specs=[pl.BlockSpec((tm, d), lambda i: (i, 0)),
                  pl.BlockSpec((tm, d), lambda i: (i, 0)),
                  pl.BlockSpec((d, d), lambda i: (0, 0)),
                  pl.BlockSpec((1, d), lambda i: (0, 0))],
        out_specs=pl.BlockSpec((tm, d), lambda i: (i, 0)),
        compiler_params=_cparams("parallel"),
        name="out_proj",
    )(merged, x, w_out, norm_w)


def _ffn_kernel(x_ref, nw_ref, wg_ref, wu_ref, wd_ref, pw_ref, o_ref, h_ref, acc_ref):
    j = pl.program_id(1)

    @pl.when(j == 0)
    def _():
        h_ref[...] = _rms(x_ref[...], nw_ref[...]).astype(BF16)
        acc_ref[...] = jnp.zeros(acc_ref.shape, F32)

    h = h_ref[...]
    gate = jnp.dot(h, wg_ref[...], preferred_element_type=F32)
    up = jnp.dot(h, wu_ref[...], preferred_element_type=F32)
    act = (jax.nn.silu(gate) * up).astype(BF16)
    acc_ref[...] += jnp.dot(act, wd_ref[...], preferred_element_type=F32)

    @pl.when(j == pl.num_programs(1) - 1)
    def _():
        o_ref[...] = x_ref[...] + _rms(acc_ref[...], pw_ref[...])


def _ffn(x, pre_w, w_gate, w_up, w_down, post_w):
    t, d = x.shape
    f = w_gate.shape[1]
    tm = min(t, 512)
    tf = 512
    return pl.pallas_call(
        _ffn_kernel,
        out_shape=jax.ShapeDtypeStruct((t, d), F32),
        grid=(t // tm, f // tf),
        in_specs=[pl.BlockSpec((tm, d), lambda i, j: (i, 0)),
                  pl.BlockSpec((1, d), lambda i, j: (0, 0)),
                  pl.BlockSpec((d, tf), lambda i, j: (0, j)),
                  pl.BlockSpec((d, tf), lambda i, j: (0, j)),
                  pl.BlockSpec((tf, d), lambda i, j: (j, 0)),
                  pl.BlockSpec((1, d), lambda i, j: (0, 0))],
        out_specs=pl.BlockSpec((tm, d), lambda i, j: (i, 0)),
        scratch_shapes=[pltpu.VMEM((tm, d), BF16), pltpu.VMEM((tm, d), F32)],
        compiler_params=_cparams("parallel", "arbitrary"),
        name="ffn",
    )(x, pre_w, w_gate, w_up, w_down, post_w)


def _swap_halves(w):
    half = w.shape[-1] // 2
    return jnp.concatenate([w[..., half:], w[..., :half]], axis=-1)


def _prep_w_in(w_in, b_gate):
    off_ssm = OFF_KPE + QK_ROPE_DIM
    off_gate = w_in.shape[1] - b_gate.shape[0]
    kpe = w_in[:, OFF_KPE:off_ssm]
    pad = jnp.zeros((w_in.shape[0], SMALL_W - OFF_KPE - 2 * QK_ROPE_DIM), w_in.dtype)
    w = jnp.concatenate([w_in[:, off_ssm:off_gate], w_in[:, :off_ssm], _swap_halves(kpe), pad,
                         w_in[:, off_gate:]], axis=1)
    bias = jnp.concatenate([jnp.zeros((w.shape[1] - b_gate.shape[0],), F32), b_gate]).reshape(1, -1)
    return w.astype(BF16), bias


def _prep_w_uq(w_uq):
    r = w_uq.shape[0]
    w = w_uq.reshape(r, MLA_HEADS, QK_NOPE_DIM + QK_ROPE_DIM)
    pe = w[..., QK_NOPE_DIM:]
    w = jnp.concatenate([w, _swap_halves(pe)], axis=-1)
    return w.reshape(r, MLA_HEADS * HEAD_PAD).astype(BF16)


def _prep_w_ukv(w_ukv):
    r = w_ukv.shape[0]
    w = w_ukv.reshape(r, MLA_HEADS, QK_NOPE_DIM + V_HEAD_DIM)
    wk = w[..., :QK_NOPE_DIM].reshape(r, MLA_HEADS * QK_NOPE_DIM)
    wv = w[..., QK_NOPE_DIM:].reshape(r, MLA_HEADS * V_HEAD_DIM)
    return wk.astype(BF16), wv.astype(BF16)


def kernel(x, positions, pre_mix_norm, w_in, b_gate, q_norm, kv_norm, w_uq, w_ukv, w_o_mla,
           ssm_a_re, ssm_a_im, ssm_log_dt, ssm_b_re, ssm_b_im, ssm_c_re, ssm_c_im, ssm_d,
           w_glu, b_glu, w_out, post_mix_norm, pre_ffn_norm, w_ffn_gate, w_ffn_up,
           w_ffn_down, post_ffn_norm):
    batch, seq, d = x.shape
    depth = w_in.shape[0]
    t = batch * seq
    ssm_w = ssm_a_re.shape[1] * SSM_GROUP
    chunks = seq // SSM_CHUNK
    assert seq % max(CHUNK, SSM_CHUNK) == 0 and batch % 8 == 0
    assert ssm_w == SMALL_W and d % SMALL_W == 0

    tab = _rope_table(positions)
    xf = x.reshape(t, d)
    row = lambda v: v.reshape(1, -1)
    for l in range(depth):
        w_in_p, bias = _prep_w_in(w_in[l], b_gate[l])
        proj, u_cb = _in_proj(xf, row(pre_mix_norm[l]), w_in_p, bias, batch=batch, seq=seq)

        wk, wv = _prep_w_ukv(w_ukv[l])
        q, k, v = _qkv_up(proj, tab, row(q_norm[l]), row(kv_norm[l]), _prep_w_uq(w_uq[l]),
                          wk, wv, batch=batch, seq=seq)
        o = _attention(q, k, v).reshape(t, MLA_HEADS * V_HEAD_DIM)

        weights = _ssm_weights(ssm_a_re[l], ssm_a_im[l], ssm_log_dt[l], ssm_b_re[l], ssm_b_im[l],
                               ssm_c_re[l], ssm_c_im[l], ssm_d[l])
        nb = u_cb.shape[0]
        z_cb = _ssm(u_cb.reshape(nb, chunks * batch, SSM_CHUNK * LANES), weights, batch=batch)
        z_cb = z_cb.reshape(u_cb.shape)

        merged = _merge(o, z_cb, proj, w_o_mla[l].astype(BF16), w_glu[l].astype(BF16),
                        row(b_glu[l]), seq=seq)
        xf = _out_proj(merged, xf, w_out[l].astype(BF16), row(post_mix_norm[l]))
        xf = _ffn(xf, row(pre_ffn_norm[l]), w_ffn_gate[l].astype(BF16), w_ffn_up[l].astype(BF16),
                  w_ffn_down[l].astype(BF16), row(post_ffn_norm[l]))
    return xf.reshape(batch, seq, d)
```

```python
import functools
import math

import jax
import jax.numpy as jnp
from jax import lax
from jax.experimental import pallas as pl
from jax.experimental.pallas import tpu as pltpu

F32 = jnp.float32
BF16 = jnp.bfloat16

CHUNK = 64
MLA_HEADS = 16
QK_NOPE_DIM = 128
QK_ROPE_DIM = 64
V_HEAD_DIM = 128
Q_LORA_RANK = 512
KV_LORA_RANK = 256
ROPE_THETA = 10000.0
SSM_GROUP = 16
SSM_STATE = 64
EPS = 1e-6

LANES = 128
HEAD_PAD = 256
SSM_CHUNK = 16
GROUPS_PER_BLOCK = LANES // SSM_GROUP
VMEM_LIMIT = 56 * 1024 * 1024
CAST_BLOCK_BYTES = 8 * 1024 * 1024
NEG_BIG = -1e30

SMALL_W = 1024
OFF_KPE = Q_LORA_RANK + KV_LORA_RANK


def _cparams(*sem):
    return pltpu.CompilerParams(dimension_semantics=sem, vmem_limit_bytes=VMEM_LIMIT)


def _rms(x, w):
    return x * lax.rsqrt(jnp.mean(x * x, axis=-1, keepdims=True) + EPS) * w


def _rope_table_kernel(pos_ref, freq_ref, tab_ref):
    ang = pos_ref[...].astype(F32) * freq_ref[...]
    lane = lax.broadcasted_iota(jnp.int32, ang.shape, 1)
    c = jnp.cos(ang)
    s = jnp.sin(ang)
    tab_ref[...] = jnp.where(lane < 64, c, jnp.where(lane < 96, -s, s))


def _rope_table(positions):
    t = positions.size
    tm = min(t, 2048)
    half = QK_ROPE_DIM // 2
    inv_freq = ROPE_THETA ** (-jnp.arange(0, QK_ROPE_DIM, 2, dtype=F32) / QK_ROPE_DIM)
    freq = jnp.tile(inv_freq, LANES // half).reshape(1, LANES)
    return pl.pallas_call(
        _rope_table_kernel,
        out_shape=jax.ShapeDtypeStruct((t, LANES), F32),
        grid=(t // tm,),
        in_specs=[pl.BlockSpec((tm, 1), lambda i: (i, 0)),
                  pl.BlockSpec((1, LANES), lambda i: (0, 0))],
        out_specs=pl.BlockSpec((tm, LANES), lambda i: (i, 0)),
        compiler_params=_cparams("parallel"),
        name="rope_table",
    )(positions.reshape(t, 1), freq)


def _in_proj_kernel(x_ref, nw_ref, w_ref, b_ref, p_ref, u_ref, h_ref):
    j = pl.program_id(1)

    @pl.when(j == 0)
    def _():
        h_ref[...] = _rms(x_ref[...], nw_ref[...]).astype(BF16)

    acc = jnp.dot(h_ref[...], w_ref[...], preferred_element_type=F32)

    gated = jax.nn.sigmoid(acc + b_ref[...])
    p_ref[...] = jnp.where(j >= 2, gated, acc).astype(BF16)

    @pl.when(j == 0)
    def _():
        u = acc.astype(BF16)
        for g in range(u_ref.shape[0]):
            for c in range(u_ref.shape[1]):
                u_ref[g, c] = u[c * SSM_CHUNK:(c + 1) * SSM_CHUNK, g * LANES:(g + 1) * LANES]


def _in_proj(x, norm_w, w, bias, *, batch, seq):
    t, d = x.shape
    n = w.shape[1]
    tn = SMALL_W
    tm = min(seq, 1024)
    per_b = seq // tm
    cpt = tm // SSM_CHUNK
    nb = tn // LANES
    return pl.pallas_call(
        _in_proj_kernel,
        out_shape=(jax.ShapeDtypeStruct((t, n - tn), BF16),
                   jax.ShapeDtypeStruct((nb, seq // SSM_CHUNK, batch * SSM_CHUNK, LANES), BF16)),
        grid=(t // tm, n // tn),
        in_specs=[pl.BlockSpec((tm, d), lambda i, j: (i, 0)),
                  pl.BlockSpec((1, d), lambda i, j: (0, 0)),
                  pl.BlockSpec((d, tn), lambda i, j: (0, j)),
                  pl.BlockSpec((1, tn), lambda i, j: (0, j))],
        out_specs=(pl.BlockSpec((tm, tn), lambda i, j: (i, jnp.maximum(j - 1, 0))),
                   pl.BlockSpec((nb, cpt, SSM_CHUNK, LANES),
                                lambda i, j: (0, i % per_b, i // per_b, 0))),
        scratch_shapes=[pltpu.VMEM((tm, d), BF16)],
        compiler_params=_cparams("parallel", "arbitrary"),
        name="in_proj",
    )(x, norm_w, w, bias)


def _qkv_up_kernel(p_ref, tab_ref, qn_ref, kvn_ref, wq_ref, wk_ref, wv_ref,
                   q_ref, k_ref, v_ref, *, scale):
    tab = tab_ref[...]
    tabs = tab * scale
    cq = _rms(p_ref[:, 0:Q_LORA_RANK].astype(F32), qn_ref[...]).astype(BF16)
    q_all = jnp.dot(cq, wq_ref[...], preferred_element_type=F32)
    for h in range(MLA_HEADS):
        base = h * HEAD_PAD
        q_ref[h, :, 0:QK_NOPE_DIM] = (q_all[:, base:base + QK_NOPE_DIM] * scale).astype(BF16)
        q_ref[h, :, QK_NOPE_DIM:HEAD_PAD] = (
            q_all[:, base + QK_NOPE_DIM:base + HEAD_PAD] * tabs).astype(BF16)

    ckv = _rms(p_ref[:, Q_LORA_RANK:OFF_KPE].astype(F32), kvn_ref[...]).astype(BF16)
    k_all = jnp.dot(ckv, wk_ref[...], preferred_element_type=F32)
    v_all = jnp.dot(ckv, wv_ref[...], preferred_element_type=F32)
    kt = p_ref[:, OFF_KPE:OFF_KPE + LANES].astype(F32) * tab
    kr = (kt + pltpu.roll(kt, LANES // 2, 1)).astype(BF16)
    ones = jnp.ones((kr.shape[0], LANES), BF16)
    for h in range(MLA_HEADS):
        k_ref[h, :, 0:QK_NOPE_DIM] = k_all[:, h * QK_NOPE_DIM:(h + 1) * QK_NOPE_DIM].astype(BF16)
        k_ref[h, :, QK_NOPE_DIM:HEAD_PAD] = kr
        v_ref[h, :, 0:V_HEAD_DIM] = v_all[:, h * V_HEAD_DIM:(h + 1) * V_HEAD_DIM].astype(BF16)
        v_ref[h, :, V_HEAD_DIM:HEAD_PAD] = ones


def _qkv_up(proj, tab, q_norm, kv_norm, wq, wk, wv, *, batch, seq):
    tm = min(seq, 512)
    nb = seq // tm
    hshape = jax.ShapeDtypeStruct((batch, MLA_HEADS, seq, HEAD_PAD), BF16)
    hspec = pl.BlockSpec((None, MLA_HEADS, tm, HEAD_PAD), lambda b, i: (b, 0, i, 0))
    scale = (QK_NOPE_DIM + QK_ROPE_DIM) ** -0.5 * math.log2(math.e)
    const = lambda b, i: (0, 0)
    return pl.pallas_call(
        functools.partial(_qkv_up_kernel, scale=scale),
        out_shape=(hshape, hshape, hshape),
        grid=(batch, nb),
        in_specs=[pl.BlockSpec((tm, SMALL_W), lambda b, i: (b * nb + i, 0)),
                  pl.BlockSpec((tm, LANES), lambda b, i: (b * nb + i, 0)),
                  pl.BlockSpec((1, Q_LORA_RANK), const),
                  pl.BlockSpec((1, KV_LORA_RANK), const),
                  pl.BlockSpec(wq.shape, const),
                  pl.BlockSpec(wk.shape, const),
                  pl.BlockSpec(wv.shape, const)],
        out_specs=(hspec, hspec, hspec),
        compiler_params=_cparams("parallel", "parallel"),
        name="qkv_up",
    )(proj, tab, q_norm, kv_norm, wq, wk, wv)


def _attn_kernel(q_ref, k_ref, v_ref, o_ref, *, tq):
    seq = q_ref.shape[0]
    per = tq // LANES
    row_chunk = lax.broadcasted_iota(jnp.int32, (tq, LANES), 0) // CHUNK
    lane = lax.broadcasted_iota(jnp.int32, (tq, LANES), 1)
    masks = [((c * LANES + lane) // CHUNK) <= row_chunk for c in range(per)]
    nq = seq // tq
    order = [t for pair in zip(range(nq - 1, -1, -1), range(nq)) for t in pair][:nq]
    for i in order:
        kv = (i + 1) * tq
        q = q_ref[i * tq:(i + 1) * tq, :]
        s = lax.dot_general(q, k_ref[0:kv, :], (((1,), (1,)), ((), ())),
                            preferred_element_type=F32)
        pieces = []
        for c in range(kv // LANES):
            piece = s[:, c * LANES:(c + 1) * LANES]
            if c >= i * per:
                piece = jnp.where(masks[c - i * per], piece, NEG_BIG)
            pieces.append(piece)
        mx = pieces[0]
        for piece in pieces[1:]:
            mx = jnp.maximum(mx, piece)
        m = jnp.broadcast_to(jnp.max(mx, axis=1, keepdims=True), (tq, LANES))
        p = jnp.concatenate([jnp.exp2(piece - m) for piece in pieces], axis=1).astype(BF16)
        acc = jnp.dot(p, v_ref[0:kv, :], preferred_element_type=F32)
        o_ref[i * tq:(i + 1) * tq, :] = (
            acc[:, 0:V_HEAD_DIM] / acc[:, V_HEAD_DIM:HEAD_PAD]).astype(o_ref.dtype)


def _attention(q, k, v):
    batch, heads, seq, _ = q.shape
    tq = min(seq, 512)
    spec = pl.BlockSpec((None, None, seq, HEAD_PAD), lambda b, h: (b, h, 0, 0))
    return pl.pallas_call(
        functools.partial(_attn_kernel, tq=tq),
        out_shape=jax.ShapeDtypeStruct((batch, seq, heads * V_HEAD_DIM), BF16),
        grid=(batch, heads),
        in_specs=[spec, spec, spec],
        out_specs=pl.BlockSpec((None, seq, V_HEAD_DIM), lambda b, h: (b, 0, h)),
        compiler_params=_cparams("parallel", "parallel"),
        name="attn",
    )(q, k, v)


def _ssm_weights(a_re, a_im, log_dt, b_re, b_im, c_re, c_im, d_skip):
    hi = lax.Precision.HIGHEST
    L = SSM_CHUNK
    g, n = a_re.shape
    p = SSM_GROUP
    gpb = GROUPS_PER_BLOCK
    nb = g // gpb
    dt = jnp.exp(log_dt)[:, None]
    mag = jnp.exp(a_re * dt)
    abar_re = mag * jnp.cos(a_im * dt)
    abar_im = mag * jnp.sin(a_im * dt)
    den = a_re * a_re + a_im * a_im
    nr = abar_re - 1.0
    f_re = (nr * a_re + abar_im * a_im) / den
    f_im = (abar_im * a_re - nr * a_im) / den
    bb_re = f_re[..., None] * b_re - f_im[..., None] * b_im
    bb_im = f_re[..., None] * b_im + f_im[..., None] * b_re
    kk = jnp.arange(L + 1, dtype=F32)[:, None, None]
    pmag = jnp.exp(kk * (a_re * dt))
    pw_re = pmag * jnp.cos(kk * (a_im * dt))
    pw_im = pmag * jnp.sin(kk * (a_im * dt))
    ca_re = c_re[None] * pw_re[:L, :, None, :] - c_im[None] * pw_im[:L, :, None, :]
    ca_im = c_re[None] * pw_im[:L, :, None, :] + c_im[None] * pw_re[:L, :, None, :]
    kern = (jnp.einsum('tgpn,gnq->tgpq', ca_re, bb_re, precision=hi)
            - jnp.einsum('tgpn,gnq->tgpq', ca_im, bb_im, precision=hi))
    eye = jnp.eye(gpb, dtype=F32)

    def block_diag(a):
        lead = a.shape[:-3]
        r, c = a.shape[-2:]
        a = a.reshape(lead + (nb, gpb, r, 1, c)) * eye[:, None, :, None]
        return a.reshape(lead + (nb, gpb * r, gpb * c))

    kern_bd = block_diag(kern.transpose(0, 1, 3, 2)).transpose(1, 0, 2, 3)
    bb_bd = (block_diag(bb_re.transpose(0, 2, 1)), block_diag(bb_im.transpose(0, 2, 1)))
    c_bd = (block_diag(c_re.transpose(0, 2, 1)), block_diag(c_im.transpose(0, 2, 1)))
    pw = jnp.concatenate([pw_re.reshape(L + 1, nb, gpb * n), pw_im.reshape(L + 1, nb, gpb * n)],
                         axis=-1).transpose(1, 0, 2)
    pw_col = jnp.concatenate([pw_re.reshape(L + 1, nb, gpb * n), pw_im.reshape(L + 1, nb, gpb * n)],
                             axis=0).transpose(1, 2, 0)
    d_row = jnp.tile(d_skip.reshape(nb, 1, LANES), (1, 1, L))
    return kern_bd.astype(BF16), bb_bd, c_bd, pw, pw_col, d_row


def _ssm_in_kernel(u_ref, bbr_ref, bbi_ref, pw_ref, v_ref, w_ref):
    @pl.when(pl.program_id(1) == 0)
    def _():
        half = bbr_ref.shape[-1]
        bbr = bbr_ref[...]
        bbi = bbi_ref[...]
        for i in range(SSM_CHUNK):
            k = SSM_CHUNK - 1 - i
            pr = pw_ref[k:k + 1, 0:half]
            pi = pw_ref[k:k + 1, half:]
            w_ref[i * LANES:(i + 1) * LANES, 0:half] = (pr * bbr - pi * bbi).astype(BF16)
            w_ref[i * LANES:(i + 1) * LANES, half:] = (pr * bbi + pi * bbr).astype(BF16)

    v_ref[...] = jnp.dot(u_ref[...], w_ref[...], preferred_element_type=F32)


def _ssm_scan_kernel(v_ref, a_ref, h_ref, *, batch, chunks):
    half = v_ref.shape[-1] // 2
    ar = jnp.broadcast_to(a_ref[:, 0:half], (batch, half))
    ai = jnp.broadcast_to(a_ref[:, half:], (batch, half))

    def step(c, carry):
        hr, hi = carry
        rows = pl.ds(pl.multiple_of(c * batch, batch), batch)
        h_ref[rows, 0:half] = hr
        h_ref[rows, half:] = hi
        return (ar * hr - ai * hi + v_ref[rows, 0:half],
                ar * hi + ai * hr + v_ref[rows, half:])

    zero = jnp.zeros((batch, half), F32)
    lax.fori_loop(0, chunks, step, (zero, zero))


def _ssm_out_kernel(u_ref, h_ref, kern_ref, cr_ref, ci_ref, pc_ref, d_ref, o_ref, t_ref, z_ref):
    @pl.when(pl.program_id(1) == 0)
    def _():
        L = SSM_CHUNK
        zero = jnp.zeros((LANES, LANES), BF16)
        for i in range(L):
            for j in range(L):
                t_ref[i * LANES:(i + 1) * LANES, j * LANES:(j + 1) * LANES] = (
                    kern_ref[j - i] if j >= i else zero)
        half = cr_ref.shape[0]
        cr = cr_ref[...]
        ci = ci_ref[...]
        for j in range(L):
            pr = pc_ref[:, j + 1:j + 2]
            pi = pc_ref[:, L + j + 2:L + j + 3]
            z_ref[0:half, j * LANES:(j + 1) * LANES] = (cr * pr - ci * pi).astype(BF16)
            z_ref[half:, j * LANES:(j + 1) * LANES] = (-(cr * pi + ci * pr)).astype(BF16)

    u = u_ref[...]
    y = (jnp.dot(u, t_ref[...], preferred_element_type=F32)
         + jnp.dot(h_ref[...].astype(BF16), z_ref[...], preferred_element_type=F32)
         + u.astype(F32) * d_ref[...])
    o_ref[...] = jax.nn.gelu(y).astype(o_ref.dtype)


def _ssm(u, weights, *, batch):
    kern_bd, (bb_r, bb_i), (c_r, c_i), pw, pw_col, d_row = weights
    nb, m, width = u.shape
    states = pw.shape[-1]
    tr = min(m, 512)
    chunks = m // batch
    grow = lambda g, r: (g, r, 0)

    def whole(a):
        return pl.BlockSpec((None,) + a.shape[1:], lambda g, r: (g,) + (0,) * (a.ndim - 1))

    v = pl.pallas_call(
        _ssm_in_kernel,
        out_shape=jax.ShapeDtypeStruct((nb, m, states), F32),
        grid=(nb, m // tr),
        in_specs=[pl.BlockSpec((None, tr, width), grow), whole(bb_r), whole(bb_i), whole(pw)],
        out_specs=pl.BlockSpec((None, tr, states), grow),
        scratch_shapes=[pltpu.VMEM((width, states), BF16)],
        compiler_params=_cparams("parallel", "arbitrary"),
        name="ssm_in",
    )(u, bb_r, bb_i, pw)

    h = pl.pallas_call(
        functools.partial(_ssm_scan_kernel, batch=batch, chunks=chunks),
        out_shape=jax.ShapeDtypeStruct((nb, m, states), F32),
        grid=(nb,),
        in_specs=[pl.BlockSpec((None, m, states), lambda g: (g, 0, 0)),
                  pl.BlockSpec((None, 1, states), lambda g: (g, 0, 0))],
        out_specs=pl.BlockSpec((None, m, states), lambda g: (g, 0, 0)),
        compiler_params=_cparams("parallel"),
        name="ssm_scan",
    )(v, pw[:, SSM_CHUNK:SSM_CHUNK + 1, :])

    return pl.pallas_call(
        _ssm_out_kernel,
        out_shape=jax.ShapeDtypeStruct(u.shape, BF16),
        grid=(nb, m // tr),
        in_specs=[pl.BlockSpec((None, tr, width), grow),
                  pl.BlockSpec((None, tr, states), grow),
                  whole(kern_bd), whole(c_r), whole(c_i), whole(pw_col), whole(d_row)],
        out_specs=pl.BlockSpec((None, tr, width), grow),
        scratch_shapes=[pltpu.VMEM((width, width), BF16), pltpu.VMEM((states, width), BF16)],
        compiler_params=_cparams("parallel", "arbitrary"),
        name="ssm_out",
    )(u, h, kern_bd, c_r, c_i, pw_col, d_row)


def _merge_kernel(o_ref, z_ref, ga_ref, gb_ref, wo_ref, w1_ref, w2_ref, b1_ref, b2_ref, m_ref):
    tm = o_ref.shape[0]
    z = jnp.concatenate([z_ref[g].reshape(tm, LANES) for g in range(z_ref.shape[0])], axis=1)
    z1 = jnp.dot(z, w1_ref[...], preferred_element_type=F32) + b1_ref[...]
    z2 = jnp.dot(z, w2_ref[...], preferred_element_type=F32) + b2_ref[...]
    s = z1 * jax.nn.sigmoid(z2)
    a = jnp.dot(o_ref[...], wo_ref[...], preferred_element_type=F32)
    m_ref[...] = (ga_ref[...].astype(F32) * a + gb_ref[...].astype(F32) * s).astype(m_ref.dtype)


def _merge(o, z_cb, proj, w_o, w_glu, b_glu, *, seq):
    t, d = o.shape
    nb = z_cb.shape[0]
    zw = nb * LANES
    tm = min(seq, 1024)
    per_b = seq // tm
    tn = 512
    nc = d // tn
    ga0 = SMALL_W // tn
    return pl.pallas_call(
        _merge_kernel,
        out_shape=jax.ShapeDtypeStruct((t, d), BF16),
        grid=(t // tm, nc),
        in_specs=[pl.BlockSpec((tm, d), lambda i, j: (i, 0)),
                  pl.BlockSpec((nb, tm // SSM_CHUNK, SSM_CHUNK, LANES),
                               lambda i, j: (0, i % per_b, i // per_b, 0)),
                  pl.BlockSpec((tm, tn), lambda i, j: (i, ga0 + j)),
                  pl.BlockSpec((tm, tn), lambda i, j: (i, ga0 + nc + j)),
                  pl.BlockSpec((d, tn), lambda i, j: (0, j)),
                  pl.BlockSpec((zw, tn), lambda i, j: (0, j)),
                  pl.BlockSpec((zw, tn), lambda i, j: (0, nc + j)),
                  pl.BlockSpec((1, tn), lambda i, j: (0, j)),
                  pl.BlockSpec((1, tn), lambda i, j: (0, nc + j))],
        out_specs=pl.BlockSpec((tm, tn), lambda i, j: (i, j)),
        compiler_params=_cparams("parallel", "arbitrary"),
        name="merge",
    )(o, z_cb, proj, proj, w_o, w_glu, w_glu, b_glu, b_glu)


def _out_proj_kernel(m_ref, x_ref, w_ref, nw_ref, o_ref):
    mix = jnp.dot(m_ref[...], w_ref[...], preferred_element_type=F32)
    o_ref[...] = x_ref[...] + _rms(mix, nw_ref[...])


def _out_proj(merged, x, w_out, norm_w):
    t, d = x.shape
    tm = min(t, 512)
    return pl.pallas_call(
        _out_proj_kernel,
        out_shape=jax.ShapeDtypeStruct((t, d), F32),
        grid=(t // tm,),
        in_specs=[pl.BlockSpec((tm, d), lambda i: (i, 0)),
                  pl.BlockSpec((tm, d), lambda i: (i, 0)),
                  pl.BlockSpec((d, d), lambda i: (0, 0)),
                  pl.BlockSpec((1, d), lambda i: (0, 0))],
        out_specs=pl.BlockSpec((tm, d), lambda i: (i, 0)),
        compiler_params=_cparams("parallel"),
        name="out_proj",
    )(merged, x, w_out, norm_w)


def _ffn_kernel(x_ref, nw_ref, wg_ref, wu_ref, wd_ref, pw_ref, o_ref, h_ref, acc_ref):
    j = pl.program_id(1)

    @pl.when(j == 0)
    def _():
        h_ref[...] = _rms(x_ref[...], nw_ref[...]).astype(BF16)
        acc_ref[...] = jnp.zeros(acc_ref.shape, F32)

    h = h_ref[...]
    gate = jnp.dot(h, wg_ref[...], preferred_element_type=F32)
    up = jnp.dot(h, wu_ref[...], preferred_element_type=F32)
    act = (jax.nn.silu(gate) * up).astype(BF16)
    acc_ref[...] += jnp.dot(act, wd_ref[...], preferred_element_type=F32)

    @pl.when(j == pl.num_programs(1) - 1)
    def _():
        o_ref[...] = x_ref[...] + _rms(acc_ref[...], pw_ref[...])


def _ffn(x, pre_w, w_gate, w_up, w_down, post_w):
    t, d = x.shape
    f = w_gate.shape[1]
    tm = min(t, 512)
    tf = 512
    return pl.pallas_call(
        _ffn_kernel,
        out_shape=jax.ShapeDtypeStruct((t, d), F32),
        grid=(t // tm, f // tf),
        in_specs=[pl.BlockSpec((tm, d), lambda i, j: (i, 0)),
                  pl.BlockSpec((1, d), lambda i, j: (0, 0)),
                  pl.BlockSpec((d, tf), lambda i, j: (0, j)),
                  pl.BlockSpec((d, tf), lambda i, j: (0, j)),
                  pl.BlockSpec((tf, d), lambda i, j: (j, 0)),
                  pl.BlockSpec((1, d), lambda i, j: (0, 0))],
        out_specs=pl.BlockSpec((tm, d), lambda i, j: (i, 0)),
        scratch_shapes=[pltpu.VMEM((tm, d), BF16), pltpu.VMEM((tm, d), F32)],
        compiler_params=_cparams("parallel", "arbitrary"),
        name="ffn",
    )(x, pre_w, w_gate, w_up, w_down, post_w)


def _cast_kernel(x_ref, o_ref):
    o_ref[...] = x_ref[...].astype(o_ref.dtype)


def _to_bf16(w):
    layers, r, c = w.shape
    tr = r
    while tr * c * 4 > CAST_BLOCK_BYTES and tr % 32 == 0:
        tr //= 2
    spec = pl.BlockSpec((None, tr, c), lambda l, i: (l, i, 0))
    return pl.pallas_call(
        _cast_kernel,
        out_shape=jax.ShapeDtypeStruct(w.shape, BF16),
        grid=(layers, r // tr),
        in_specs=[spec],
        out_specs=spec,
        compiler_params=_cparams("parallel", "parallel"),
        name="to_bf16",
    )(w)


def _swap_halves(w):
    half = w.shape[-1] // 2
    return jnp.concatenate([w[..., half:], w[..., :half]], axis=-1)


def _prep_w_in(w_in, b_gate):
    off_ssm = OFF_KPE + QK_ROPE_DIM
    off_gate = w_in.shape[-1] - b_gate.shape[-1]
    kpe = w_in[..., OFF_KPE:off_ssm]
    pad = jnp.zeros(w_in.shape[:-1] + (SMALL_W - OFF_KPE - 2 * QK_ROPE_DIM,), w_in.dtype)
    w = jnp.concatenate([w_in[..., off_ssm:off_gate], w_in[..., :off_ssm], _swap_halves(kpe), pad,
                         w_in[..., off_gate:]], axis=-1)
    zeros = jnp.zeros(b_gate.shape[:-1] + (w.shape[-1] - b_gate.shape[-1],), F32)
    return w, jnp.concatenate([zeros, b_gate], axis=-1)


def _prep_w_uq(w_uq):
    lead = w_uq.shape[:-1]
    w = w_uq.reshape(lead + (MLA_HEADS, QK_NOPE_DIM + QK_ROPE_DIM))
    pe = w[..., QK_NOPE_DIM:]
    w = jnp.concatenate([w, _swap_halves(pe)], axis=-1)
    return w.reshape(lead + (MLA_HEADS * HEAD_PAD,))


def _prep_w_ukv(w_ukv):
    lead = w_ukv.shape[:-1]
    w = w_ukv.reshape(lead + (MLA_HEADS, QK_NOPE_DIM + V_HEAD_DIM))
    wk = w[..., :QK_NOPE_DIM].reshape(lead + (MLA_HEADS * QK_NOPE_DIM,))
    wv = w[..., QK_NOPE_DIM:].reshape(lead + (MLA_HEADS * V_HEAD_DIM,))
    return wk, wv


def kernel(x, positions, pre_mix_norm, w_in, b_gate, q_norm, kv_norm, w_uq, w_ukv, w_o_mla,
           ssm_a_re, ssm_a_im, ssm_log_dt, ssm_b_re, ssm_b_im, ssm_c_re, ssm_c_im, ssm_d,
           w_glu, b_glu, w_out, post_mix_norm, pre_ffn_norm, w_ffn_gate, w_ffn_up,
           w_ffn_down, post_ffn_norm):
    batch, seq, d = x.shape
    depth = w_in.shape[0]
    t = batch * seq
    ssm_w = ssm_a_re.shape[1] * SSM_GROUP
    chunks = seq // SSM_CHUNK
    assert seq % max(CHUNK, SSM_CHUNK) == 0 and batch % 8 == 0
    assert ssm_w == SMALL_W and d % SMALL_W == 0

    w_in_p, bias = _prep_w_in(_to_bf16(w_in), b_gate)
    wq = _prep_w_uq(_to_bf16(w_uq))
    wk, wv = _prep_w_ukv(_to_bf16(w_ukv))
    w_o_b, w_glu_b, w_out_b = _to_bf16(w_o_mla), _to_bf16(w_glu), _to_bf16(w_out)
    w_fg, w_fu, w_fd = _to_bf16(w_ffn_gate), _to_bf16(w_ffn_up), _to_bf16(w_ffn_down)

    tab = _rope_table(positions)
    xf = x.reshape(t, d)
    row = lambda v: v.reshape(1, -1)
    for l in range(depth):
        proj, u_cb = _in_proj(xf, row(pre_mix_norm[l]), w_in_p[l], row(bias[l]),
                              batch=batch, seq=seq)
        q, k, v = _qkv_up(proj, tab, row(q_norm[l]), row(kv_norm[l]), wq[l], wk[l], wv[l],
                          batch=batch, seq=seq)
        o = _attention(q, k, v).reshape(t, MLA_HEADS * V_HEAD_DIM)

        weights = _ssm_weights(ssm_a_re[l], ssm_a_im[l], ssm_log_dt[l], ssm_b_re[l], ssm_b_im[l],
                               ssm_c_re[l], ssm_c_im[l], ssm_d[l])
        nb = u_cb.shape[0]
        z_cb = _ssm(u_cb.reshape(nb, chunks * batch, SSM_CHUNK * LANES), weights, batch=batch)
        z_cb = z_cb.reshape(u_cb.shape)

        merged = _merge(o, z_cb, proj, w_o_b[l], w_glu_b[l], row(b_glu[l]), seq=seq)
        xf = _out_proj(merged, xf, w_out_b[l], row(post_mix_norm[l]))
        xf = _ffn(xf, row(pre_ffn_norm[l]), w_fg[l], w_fu[l], w_fd[l], row(post_ffn_norm[l]))
    return xf.reshape(batch, seq, d)
```

```python
import functools
import math

import jax
import jax.numpy as jnp
from jax import lax
from jax.experimental import pallas as pl
from jax.experimental.pallas import tpu as pltpu

F32 = jnp.float32
BF16 = jnp.bfloat16

CHUNK = 64
MLA_HEADS = 16
QK_NOPE_DIM = 128
QK_ROPE_DIM = 64
V_HEAD_DIM = 128
Q_LORA_RANK = 512
KV_LORA_RANK = 256
ROPE_THETA = 10000.0
SSM_GROUP = 16
SSM_STATE = 64
EPS = 1e-6

LANES = 128
HEAD_PAD = 256
SSM_CHUNK = 16
GROUPS_PER_BLOCK = LANES // SSM_GROUP
VMEM_LIMIT = 56 * 1024 * 1024
CAST_BLOCK_BYTES = 8 * 1024 * 1024
NEG_BIG = -1e30

SMALL_W = 1024
OFF_KPE = Q_LORA_RANK + KV_LORA_RANK


def _cparams(*sem):
    return pltpu.CompilerParams(dimension_semantics=sem, vmem_limit_bytes=VMEM_LIMIT)


def _rms(x, w):
    return x * lax.rsqrt(jnp.mean(x * x, axis=-1, keepdims=True) + EPS) * w


def _rope_table_kernel(pos_ref, freq_ref, tab_ref):
    ang = pos_ref[...].astype(F32) * freq_ref[...]
    lane = lax.broadcasted_iota(jnp.int32, ang.shape, 1)
    c = jnp.cos(ang)
    s = jnp.sin(ang)
    tab_ref[...] = jnp.where(lane < 64, c, jnp.where(lane < 96, -s, s))


def _rope_table(positions):
    t = positions.size
    tm = min(t, 2048)
    half = QK_ROPE_DIM // 2
    inv_freq = ROPE_THETA ** (-jnp.arange(0, QK_ROPE_DIM, 2, dtype=F32) / QK_ROPE_DIM)
    freq = jnp.tile(inv_freq, LANES // half).reshape(1, LANES)
    return pl.pallas_call(
        _rope_table_kernel,
        out_shape=jax.ShapeDtypeStruct((t, LANES), F32),
        grid=(t // tm,),
        in_specs=[pl.BlockSpec((tm, 1), lambda i: (i, 0)),
                  pl.BlockSpec((1, LANES), lambda i: (0, 0))],
        out_specs=pl.BlockSpec((tm, LANES), lambda i: (i, 0)),
        compiler_params=_cparams("parallel"),
        name="rope_table",
    )(positions.reshape(t, 1), freq)


def _in_proj_kernel(x_ref, nw_ref, w_ref, b_ref, p_ref, u_ref, h_ref):
    j = pl.program_id(1)

    @pl.when(j == 0)
    def _():
        h_ref[...] = _rms(x_ref[...], nw_ref[...]).astype(BF16)

    acc = jnp.dot(h_ref[...], w_ref[...], preferred_element_type=F32)

    gated = jax.nn.sigmoid(acc + b_ref[...])
    p_ref[...] = jnp.where(j >= 2, gated, acc).astype(BF16)

    @pl.when(j == 0)
    def _():
        u = acc.astype(BF16)
        for g in range(u_ref.shape[0]):
            for c in range(u_ref.shape[1]):
                u_ref[g, c] = u[c * SSM_CHUNK:(c + 1) * SSM_CHUNK, g * LANES:(g + 1) * LANES]


def _in_proj(x, norm_w, w, bias, *, layer, batch, seq):
    t, d = x.shape
    n = w.shape[2]
    tn = SMALL_W
    tm = min(seq, 1024)
    per_b = seq // tm
    cpt = tm // SSM_CHUNK
    nb = tn // LANES
    return pl.pallas_call(
        _in_proj_kernel,
        out_shape=(jax.ShapeDtypeStruct((t, n - tn), BF16),
                   jax.ShapeDtypeStruct((nb, seq // SSM_CHUNK, batch * SSM_CHUNK, LANES), BF16)),
        grid=(t // tm, n // tn),
        in_specs=[pl.BlockSpec((tm, d), lambda i, j: (i, 0)),
                  pl.BlockSpec((1, d), lambda i, j: (0, 0)),
                  pl.BlockSpec((None, d, tn), lambda i, j: (layer, 0, j)),
                  pl.BlockSpec((1, tn), lambda i, j: (0, j))],
        out_specs=(pl.BlockSpec((tm, tn), lambda i, j: (i, jnp.maximum(j - 1, 0))),
                   pl.BlockSpec((nb, cpt, SSM_CHUNK, LANES),
                                lambda i, j: (0, i % per_b, i // per_b, 0))),
        scratch_shapes=[pltpu.VMEM((tm, d), BF16)],
        compiler_params=_cparams("parallel", "arbitrary"),
        name="in_proj",
    )(x, norm_w, w, bias)


def _qkv_up_kernel(p_ref, tab_ref, qn_ref, kvn_ref, wq_ref, wk_ref, wv_ref,
                   q_ref, k_ref, v_ref, *, scale):
    tab = tab_ref[...]
    tabs = tab * scale
    cq = _rms(p_ref[:, 0:Q_LORA_RANK].astype(F32), qn_ref[...]).astype(BF16)
    q_all = jnp.dot(cq, wq_ref[...], preferred_element_type=F32)
    for h in range(MLA_HEADS):
        base = h * HEAD_PAD
        q_ref[h, :, 0:QK_NOPE_DIM] = (q_all[:, base:base + QK_NOPE_DIM] * scale).astype(BF16)
        q_ref[h, :, QK_NOPE_DIM:HEAD_PAD] = (
            q_all[:, base + QK_NOPE_DIM:base + HEAD_PAD] * tabs).astype(BF16)

    ckv = _rms(p_ref[:, Q_LORA_RANK:OFF_KPE].astype(F32), kvn_ref[...]).astype(BF16)
    k_all = jnp.dot(ckv, wk_ref[...], preferred_element_type=F32)
    v_all = jnp.dot(ckv, wv_ref[...], preferred_element_type=F32)
    kt = p_ref[:, OFF_KPE:OFF_KPE + LANES].astype(F32) * tab
    kr = (kt + pltpu.roll(kt, LANES // 2, 1)).astype(BF16)
    ones = jnp.ones((kr.shape[0], LANES), BF16)
    for h in range(MLA_HEADS):
        k_ref[h, :, 0:QK_NOPE_DIM] = k_all[:, h * QK_NOPE_DIM:(h + 1) * QK_NOPE_DIM].astype(BF16)
        k_ref[h, :, QK_NOPE_DIM:HEAD_PAD] = kr
        v_ref[h, :, 0:V_HEAD_DIM] = v_all[:, h * V_HEAD_DIM:(h + 1) * V_HEAD_DIM].astype(BF16)
        v_ref[h, :, V_HEAD_DIM:HEAD_PAD] = ones


def _qkv_up(proj, tab, q_norm, kv_norm, wq, wk, wv, *, batch, seq):
    tm = min(seq, 512)
    nb = seq // tm
    hshape = jax.ShapeDtypeStruct((batch, MLA_HEADS, seq, HEAD_PAD), BF16)
    hspec = pl.BlockSpec((None, MLA_HEADS, tm, HEAD_PAD), lambda b, i: (b, 0, i, 0))
    scale = (QK_NOPE_DIM + QK_ROPE_DIM) ** -0.5 * math.log2(math.e)
    const = lambda b, i: (0, 0)
    return pl.pallas_call(
        functools.partial(_qkv_up_kernel, scale=scale),
        out_shape=(hshape, hshape, hshape),
        grid=(batch, nb),
        in_specs=[pl.BlockSpec((tm, SMALL_W), lambda b, i: (b * nb + i, 0)),
                  pl.BlockSpec((tm, LANES), lambda b, i: (b * nb + i, 0)),
                  pl.BlockSpec((1, Q_LORA_RANK), const),
                  pl.BlockSpec((1, KV_LORA_RANK), const),
                  pl.BlockSpec(wq.shape, const),
                  pl.BlockSpec(wk.shape, const),
                  pl.BlockSpec(wv.shape, const)],
        out_specs=(hspec, hspec, hspec),
        compiler_params=_cparams("parallel", "parallel"),
        name="qkv_up",
    )(proj, tab, q_norm, kv_norm, wq, wk, wv)


def _attn_kernel(q_ref, k_ref, v_ref, o_ref, *, tq):
    seq = q_ref.shape[0]
    per = tq // LANES
    row_chunk = lax.broadcasted_iota(jnp.int32, (tq, LANES), 0) // CHUNK
    lane = lax.broadcasted_iota(jnp.int32, (tq, LANES), 1)
    masks = [((c * LANES + lane) // CHUNK) <= row_chunk for c in range(per)]
    nq = seq // tq
    order = [t for pair in zip(range(nq - 1, -1, -1), range(nq)) for t in pair][:nq]
    for i in order:
        kv = (i + 1) * tq
        q = q_ref[i * tq:(i + 1) * tq, :]
        s = lax.dot_general(q, k_ref[0:kv, :], (((1,), (1,)), ((), ())),
                            preferred_element_type=F32)
        pieces = []
        for c in range(kv // LANES):
            piece = s[:, c * LANES:(c + 1) * LANES]
            if c >= i * per:
                piece = jnp.where(masks[c - i * per], piece, NEG_BIG)
            pieces.append(piece)
        mx = pieces[0]
        for piece in pieces[1:]:
            mx = jnp.maximum(mx, piece)
        m = jnp.broadcast_to(jnp.max(mx, axis=1, keepdims=True), (tq, LANES))
        p = jnp.concatenate([jnp.exp2(piece - m) for piece in pieces], axis=1).astype(BF16)
        acc = jnp.dot(p, v_ref[0:kv, :], preferred_element_type=F32)
        o_ref[i * tq:(i + 1) * tq, :] = (
            acc[:, 0:V_HEAD_DIM] / acc[:, V_HEAD_DIM:HEAD_PAD]).astype(o_ref.dtype)


def _attention(q, k, v):
    batch, heads, seq, _ = q.shape
    tq = min(seq, 512)
    spec = pl.BlockSpec((None, None, seq, HEAD_PAD), lambda b, h: (b, h, 0, 0))
    return pl.pallas_call(
        functools.partial(_attn_kernel, tq=tq),
        out_shape=jax.ShapeDtypeStruct((batch, seq, heads * V_HEAD_DIM), BF16),
        grid=(batch, heads),
        in_specs=[spec, spec, spec],
        out_specs=pl.BlockSpec((None, seq, V_HEAD_DIM), lambda b, h: (b, 0, h)),
        compiler_params=_cparams("parallel", "parallel"),
        name="attn",
    )(q, k, v)


def _ssm_weights(a_re, a_im, log_dt, b_re, b_im, c_re, c_im, d_skip):
    hi = lax.Precision.HIGHEST
    L = SSM_CHUNK
    g, n = a_re.shape
    p = SSM_GROUP
    gpb = GROUPS_PER_BLOCK
    nb = g // gpb
    dt = jnp.exp(log_dt)[:, None]
    mag = jnp.exp(a_re * dt)
    abar_re = mag * jnp.cos(a_im * dt)
    abar_im = mag * jnp.sin(a_im * dt)
    den = a_re * a_re + a_im * a_im
    nr = abar_re - 1.0
    f_re = (nr * a_re + abar_im * a_im) / den
    f_im = (abar_im * a_re - nr * a_im) / den
    bb_re = f_re[..., None] * b_re - f_im[..., None] * b_im
    bb_im = f_re[..., None] * b_im + f_im[..., None] * b_re
    kk = jnp.arange(L + 1, dtype=F32)[:, None, None]
    pmag = jnp.exp(kk * (a_re * dt))
    pw_re = pmag * jnp.cos(kk * (a_im * dt))
    pw_im = pmag * jnp.sin(kk * (a_im * dt))
    ca_re = c_re[None] * pw_re[:L, :, None, :] - c_im[None] * pw_im[:L, :, None, :]
    ca_im = c_re[None] * pw_im[:L, :, None, :] + c_im[None] * pw_re[:L, :, None, :]
    kern = (jnp.einsum('tgpn,gnq->tgpq', ca_re, bb_re, precision=hi)
            - jnp.einsum('tgpn,gnq->tgpq', ca_im, bb_im, precision=hi))
    eye = jnp.eye(gpb, dtype=F32)

    def block_diag(a):
        lead = a.shape[:-3]
        r, c = a.shape[-2:]
        a = a.reshape(lead + (nb, gpb, r, 1, c)) * eye[:, None, :, None]
        return a.reshape(lead + (nb, gpb * r, gpb * c))

    kern_bd = block_diag(kern.transpose(0, 1, 3, 2)).transpose(1, 0, 2, 3)
    bb_bd = (block_diag(bb_re.transpose(0, 2, 1)), block_diag(bb_im.transpose(0, 2, 1)))
    c_bd = (block_diag(c_re.transpose(0, 2, 1)), block_diag(c_im.transpose(0, 2, 1)))
    pw = jnp.concatenate([pw_re.reshape(L + 1, nb, gpb * n), pw_im.reshape(L + 1, nb, gpb * n)],
                         axis=-1).transpose(1, 0, 2)
    pw_col = jnp.concatenate([pw_re.reshape(L + 1, nb, gpb * n), pw_im.reshape(L + 1, nb, gpb * n)],
                             axis=0).transpose(1, 2, 0)
    d_row = jnp.tile(d_skip.reshape(nb, 1, LANES), (1, 1, L))
    return kern_bd.astype(BF16), bb_bd, c_bd, pw, pw_col, d_row


def _steps_to_lanes(u_ref, stage_ref):
    stage_ref[...] = u_ref[...].astype(F32)
    rows = stage_ref.shape[0] // SSM_CHUNK
    return jnp.concatenate([stage_ref[pl.ds(i, rows, stride=SSM_CHUNK), :]
                            for i in range(SSM_CHUNK)], axis=1)


def _ssm_in_kernel(u_ref, bbr_ref, bbi_ref, pw_ref, v_ref, w_ref, stage_ref):
    @pl.when(pl.program_id(1) == 0)
    def _():
        half = bbr_ref.shape[-1]
        bbr = bbr_ref[...]
        bbi = bbi_ref[...]
        for i in range(SSM_CHUNK):
            k = SSM_CHUNK - 1 - i
            pr = pw_ref[k:k + 1, 0:half]
            pi = pw_ref[k:k + 1, half:]
            w_ref[i * LANES:(i + 1) * LANES, 0:half] = (pr * bbr - pi * bbi).astype(BF16)
            w_ref[i * LANES:(i + 1) * LANES, half:] = (pr * bbi + pi * bbr).astype(BF16)

    u = _steps_to_lanes(u_ref, stage_ref).astype(BF16)
    v_ref[...] = jnp.dot(u, w_ref[...], preferred_element_type=F32)


def _ssm_scan_kernel(v_ref, a_ref, h_ref, *, batch, chunks):
    half = v_ref.shape[-1] // 2
    ar = jnp.broadcast_to(a_ref[:, 0:half], (batch, half))
    ai = jnp.broadcast_to(a_ref[:, half:], (batch, half))

    def step(c, carry):
        hr, hi = carry
        rows = pl.ds(pl.multiple_of(c * batch, batch), batch)
        h_ref[rows, 0:half] = hr
        h_ref[rows, half:] = hi
        return (ar * hr - ai * hi + v_ref[rows, 0:half],
                ar * hi + ai * hr + v_ref[rows, half:])

    zero = jnp.zeros((batch, half), F32)
    lax.fori_loop(0, chunks, step, (zero, zero))


def _ssm_out_kernel(u_ref, h_ref, kern_ref, cr_ref, ci_ref, pc_ref, d_ref, o_ref,
                    t_ref, z_ref, stage_ref):
    @pl.when(pl.program_id(1) == 0)
    def _():
        L = SSM_CHUNK
        zero = jnp.zeros((LANES, LANES), BF16)
        for i in range(L):
            for j in range(L):
                t_ref[i * LANES:(i + 1) * LANES, j * LANES:(j + 1) * LANES] = (
                    kern_ref[j - i] if j >= i else zero)
        half = cr_ref.shape[0]
        cr = cr_ref[...]
        ci = ci_ref[...]
        for j in range(L):
            pr = pc_ref[:, j + 1:j + 2]
            pi = pc_ref[:, L + j + 2:L + j + 3]
            z_ref[0:half, j * LANES:(j + 1) * LANES] = (cr * pr - ci * pi).astype(BF16)
            z_ref[half:, j * LANES:(j + 1) * LANES] = (-(cr * pi + ci * pr)).astype(BF16)

    u = _steps_to_lanes(u_ref, stage_ref)
    y = (jnp.dot(u.astype(BF16), t_ref[...], preferred_element_type=F32)
         + jnp.dot(h_ref[...].astype(BF16), z_ref[...], preferred_element_type=F32)
         + u * d_ref[...])
    z = jax.nn.gelu(y)
    rows = z.shape[0]
    for i in range(SSM_CHUNK):
        stage_ref[pl.ds(i, rows, stride=SSM_CHUNK), :] = z[:, i * LANES:(i + 1) * LANES]
    o_ref[...] = stage_ref[...].astype(o_ref.dtype)


def _ssm(u, weights, *, batch):
    kern_bd, (bb_r, bb_i), (c_r, c_i), pw, pw_col, d_row = weights
    nb, tokens, _ = u.shape
    m = tokens // SSM_CHUNK
    width = SSM_CHUNK * LANES
    states = pw.shape[-1]
    tr = min(m, 512)
    chunks = m // batch
    grow = lambda g, r: (g, r, 0)
    u_spec = pl.BlockSpec((None, tr * SSM_CHUNK, LANES), grow)
    stage = pltpu.VMEM((tr * SSM_CHUNK, LANES), F32)

    def whole(a):
        return pl.BlockSpec((None,) + a.shape[1:], lambda g, r: (g,) + (0,) * (a.ndim - 1))

    v = pl.pallas_call(
        _ssm_in_kernel,
        out_shape=jax.ShapeDtypeStruct((nb, m, states), F32),
        grid=(nb, m // tr),
        in_specs=[u_spec, whole(bb_r), whole(bb_i), whole(pw)],
        out_specs=pl.BlockSpec((None, tr, states), grow),
        scratch_shapes=[pltpu.VMEM((width, states), BF16), stage],
        compiler_params=_cparams("parallel", "arbitrary"),
        name="ssm_in",
    )(u, bb_r, bb_i, pw)

    h = pl.pallas_call(
        functools.partial(_ssm_scan_kernel, batch=batch, chunks=chunks),
        out_shape=jax.ShapeDtypeStruct((nb, m, states), F32),
        grid=(nb,),
        in_specs=[pl.BlockSpec((None, m, states), lambda g: (g, 0, 0)),
                  pl.BlockSpec((None, 1, states), lambda g: (g, 0, 0))],
        out_specs=pl.BlockSpec((None, m, states), lambda g: (g, 0, 0)),
        compiler_params=_cparams("parallel"),
        name="ssm_scan",
    )(v, pw[:, SSM_CHUNK:SSM_CHUNK + 1, :])

    return pl.pallas_call(
        _ssm_out_kernel,
        out_shape=jax.ShapeDtypeStruct(u.shape, BF16),
        grid=(nb, m // tr),
        in_specs=[u_spec,
                  pl.BlockSpec((None, tr, states), grow),
                  whole(kern_bd), whole(c_r), whole(c_i), whole(pw_col), whole(d_row)],
        out_specs=u_spec,
        scratch_shapes=[pltpu.VMEM((width, width), BF16), pltpu.VMEM((states, width), BF16),
                        stage],
        compiler_params=_cparams("parallel", "arbitrary"),
        name="ssm_out",
    )(u, h, kern_bd, c_r, c_i, pw_col, d_row)


def _merge_kernel(o_ref, z_ref, ga_ref, gb_ref, wo_ref, w1_ref, w2_ref, b1_ref, b2_ref, m_ref):
    tm = o_ref.shape[0]
    z = jnp.concatenate([z_ref[g].reshape(tm, LANES) for g in range(z_ref.shape[0])], axis=1)
    z1 = jnp.dot(z, w1_ref[...], preferred_element_type=F32) + b1_ref[...]
    z2 = jnp.dot(z, w2_ref[...], preferred_element_type=F32) + b2_ref[...]
    s = z1 * jax.nn.sigmoid(z2)
    a = jnp.dot(o_ref[...], wo_ref[...], preferred_element_type=F32)
    m_ref[...] = (ga_ref[...].astype(F32) * a + gb_ref[...].astype(F32) * s).astype(m_ref.dtype)


def _merge(o, z_cb, proj, w_o, w_glu, b_glu, *, layer, seq):
    t, d = o.shape
    nb = z_cb.shape[0]
    zw = nb * LANES
    tm = min(seq, 1024)
    per_b = seq // tm
    tn = 512
    nc = d // tn
    ga0 = SMALL_W // tn
    return pl.pallas_call(
        _merge_kernel,
        out_shape=jax.ShapeDtypeStruct((t, d), BF16),
        grid=(t // tm, nc),
        in_specs=[pl.BlockSpec((tm, d), lambda i, j: (i, 0)),
                  pl.BlockSpec((nb, tm // SSM_CHUNK, SSM_CHUNK, LANES),
                               lambda i, j: (0, i % per_b, i // per_b, 0)),
                  pl.BlockSpec((tm, tn), lambda i, j: (i, ga0 + j)),
                  pl.BlockSpec((tm, tn), lambda i, j: (i, ga0 + nc + j)),
                  pl.BlockSpec((None, d, tn), lambda i, j: (layer, 0, j)),
                  pl.BlockSpec((None, zw, tn), lambda i, j: (layer, 0, j)),
                  pl.BlockSpec((None, zw, tn), lambda i, j: (layer, 0, nc + j)),
                  pl.BlockSpec((1, tn), lambda i, j: (0, j)),
                  pl.BlockSpec((1, tn), lambda i, j: (0, nc + j))],
        out_specs=pl.BlockSpec((tm, tn), lambda i, j: (i, j)),
        compiler_params=_cparams("parallel", "arbitrary"),
        name="merge",
    )(o, z_cb, proj, proj, w_o, w_glu, w_glu, b_glu, b_glu)


def _out_proj_kernel(m_ref, x_ref, w_ref, nw_ref, o_ref):
    mix = jnp.dot(m_ref[...], w_ref[...], preferred_element_type=F32)
    o_ref[...] = x_ref[...] + _rms(mix, nw_ref[...])


def _out_proj(merged, x, w_out, norm_w, *, layer):
    t, d = x.shape
    tm = min(t, 512)
    return pl.pallas_call(
        _out_proj_kernel,
        out_shape=jax.ShapeDtypeStruct((t, d), F32),
        grid=(t // tm,),
        in_specs=[pl.BlockSpec((tm, d), lambda i: (i, 0)),
                  pl.BlockSpec((tm, d), lambda i: (i, 0)),
                  pl.BlockSpec((None, d, d), lambda i: (layer, 0, 0)),
                  pl.BlockSpec((1, d), lambda i: (0, 0))],
        out_specs=pl.BlockSpec((tm, d), lambda i: (i, 0)),
        compiler_params=_cparams("parallel"),
        name="out_proj",
    )(merged, x, w_out, norm_w)


def _ffn_kernel(x_ref, nw_ref, wg_ref, wu_ref, wd_ref, pw_ref, o_ref, h_ref, acc_ref):
    j = pl.program_id(1)

    @pl.when(j == 0)
    def _():
        h_ref[...] = _rms(x_ref[...], nw_ref[...]).astype(BF16)
        acc_ref[...] = jnp.zeros(acc_ref.shape, F32)

    h = h_ref[...]
    gate = jnp.dot(h, wg_ref[...], preferred_element_type=F32)
    up = jnp.dot(h, wu_ref[...], preferred_element_type=F32)
    act = (jax.nn.silu(gate) * up).astype(BF16)
    acc_ref[...] += jnp.dot(act, wd_ref[...], preferred_element_type=F32)

    @pl.when(j == pl.num_programs(1) - 1)
    def _():
        o_ref[...] = x_ref[...] + _rms(acc_ref[...], pw_ref[...])


def _ffn(x, pre_w, w_gate, w_up, w_down, post_w, *, layer):
    t, d = x.shape
    f = w_gate.shape[2]
    tm = min(t, 512)
    tf = 512
    return pl.pallas_call(
        _ffn_kernel,
        out_shape=jax.ShapeDtypeStruct((t, d), F32),
        grid=(t // tm, f // tf),
        in_specs=[pl.BlockSpec((tm, d), lambda i, j: (i, 0)),
                  pl.BlockSpec((1, d), lambda i, j: (0, 0)),
                  pl.BlockSpec((None, d, tf), lambda i, j: (layer, 0, j)),
                  pl.BlockSpec((None, d, tf), lambda i, j: (layer, 0, j)),
                  pl.BlockSpec((None, tf, d), lambda i, j: (layer, j, 0)),
                  pl.BlockSpec((1, d), lambda i, j: (0, 0))],
        out_specs=pl.BlockSpec((tm, d), lambda i, j: (i, 0)),
        scratch_shapes=[pltpu.VMEM((tm, d), BF16), pltpu.VMEM((tm, d), F32)],
        compiler_params=_cparams("parallel", "arbitrary"),
        name="ffn",
    )(x, pre_w, w_gate, w_up, w_down, post_w)


def _cast_kernel(x_ref, o_ref):
    o_ref[...] = x_ref[...].astype(o_ref.dtype)


def _to_bf16(w):
    layers, r, c = w.shape
    tr = r
    while tr * c * 4 > CAST_BLOCK_BYTES and tr % 32 == 0:
        tr //= 2
    spec = pl.BlockSpec((None, tr, c), lambda l, i: (l, i, 0))
    return pl.pallas_call(
        _cast_kernel,
        out_shape=jax.ShapeDtypeStruct(w.shape, BF16),
        grid=(layers, r // tr),
        in_specs=[spec],
        out_specs=spec,
        compiler_params=_cparams("parallel", "parallel"),
        name="to_bf16",
    )(w)


def _swap_halves(w):
    half = w.shape[-1] // 2
    return jnp.concatenate([w[..., half:], w[..., :half]], axis=-1)


def _prep_w_in(w_in, b_gate):
    off_ssm = OFF_KPE + QK_ROPE_DIM
    off_gate = w_in.shape[-1] - b_gate.shape[-1]
    kpe = w_in[..., OFF_KPE:off_ssm]
    pad = jnp.zeros(w_in.shape[:-1] + (SMALL_W - OFF_KPE - 2 * QK_ROPE_DIM,), w_in.dtype)
    w = jnp.concatenate([w_in[..., off_ssm:off_gate], w_in[..., :off_ssm], _swap_halves(kpe), pad,
                         w_in[..., off_gate:]], axis=-1)
    zeros = jnp.zeros(b_gate.shape[:-1] + (w.shape[-1] - b_gate.shape[-1],), F32)
    return w, jnp.concatenate([zeros, b_gate], axis=-1)


def _prep_w_uq(w_uq):
    lead = w_uq.shape[:-1]
    w = w_uq.reshape(lead + (MLA_HEADS, QK_NOPE_DIM + QK_ROPE_DIM))
    pe = w[..., QK_NOPE_DIM:]
    w = jnp.concatenate([w, _swap_halves(pe)], axis=-1)
    return w.reshape(lead + (MLA_HEADS * HEAD_PAD,))


def _prep_w_ukv(w_ukv):
    lead = w_ukv.shape[:-1]
    w = w_ukv.reshape(lead + (MLA_HEADS, QK_NOPE_DIM + V_HEAD_DIM))
    wk = w[..., :QK_NOPE_DIM].reshape(lead + (MLA_HEADS * QK_NOPE_DIM,))
    wv = w[..., QK_NOPE_DIM:].reshape(lead + (MLA_HEADS * V_HEAD_DIM,))
    return wk, wv


def kernel(x, positions, pre_mix_norm, w_in, b_gate, q_norm, kv_norm, w_uq, w_ukv, w_o_mla,
           ssm_a_re, ssm_a_im, ssm_log_dt, ssm_b_re, ssm_b_im, ssm_c_re, ssm_c_im, ssm_d,
           w_glu, b_glu, w_out, post_mix_norm, pre_ffn_norm, w_ffn_gate, w_ffn_up,
           w_ffn_down, post_ffn_norm):
    batch, seq, d = x.shape
    depth = w_in.shape[0]
    t = batch * seq
    ssm_w = ssm_a_re.shape[1] * SSM_GROUP
    chunks = seq // SSM_CHUNK
    assert seq % max(CHUNK, SSM_CHUNK) == 0 and batch % 8 == 0
    assert ssm_w == SMALL_W and d % SMALL_W == 0

    w_in_p, bias = _prep_w_in(_to_bf16(w_in), b_gate)
    wq = _prep_w_uq(_to_bf16(w_uq))
    wk, wv = _prep_w_ukv(_to_bf16(w_ukv))
    w_o_b, w_glu_b, w_out_b = _to_bf16(w_o_mla), _to_bf16(w_glu), _to_bf16(w_out)
    w_fg, w_fu, w_fd = _to_bf16(w_ffn_gate), _to_bf16(w_ffn_up), _to_bf16(w_ffn_down)
    ssm_weights = jax.vmap(_ssm_weights)(ssm_a_re, ssm_a_im, ssm_log_dt, ssm_b_re, ssm_b_im,
                                         ssm_c_re, ssm_c_im, ssm_d)

    tab = _rope_table(positions)
    xf = x.reshape(t, d)
    row = lambda v: v.reshape(1, -1)
    for l in range(depth):
        proj, u_cb = _in_proj(xf, row(pre_mix_norm[l]), w_in_p, row(bias[l]),
                              layer=l, batch=batch, seq=seq)
        q, k, v = _qkv_up(proj, tab, row(q_norm[l]), row(kv_norm[l]), wq[l], wk[l], wv[l],
                          batch=batch, seq=seq)
        o = _attention(q, k, v).reshape(t, MLA_HEADS * V_HEAD_DIM)

        weights = jax.tree.map(lambda a: a[l], ssm_weights)
        nb = u_cb.shape[0]
        z_cb = _ssm(u_cb.reshape(nb, chunks * batch * SSM_CHUNK, LANES), weights, batch=batch)
        z_cb = z_cb.reshape(u_cb.shape)

        merged = _merge(o, z_cb, proj, w_o_b, w_glu_b, row(b_glu[l]), layer=l, seq=seq)
        xf = _out_proj(merged, xf, w_out_b, row(post_mix_norm[l]), layer=l)
        xf = _ffn(xf, row(pre_ffn_norm[l]), w_fg, w_fu, w_fd, row(post_ffn_norm[l]), layer=l)
    return xf.reshape(batch, seq, d)
```

```python
import functools
import math

import jax
import jax.numpy as jnp
from jax import lax
from jax.experimental import pallas as pl
from jax.experimental.pallas import tpu as pltpu

F32 = jnp.float32
BF16 = jnp.bfloat16

CHUNK = 64
MLA_HEADS = 16
QK_NOPE_DIM = 128
QK_ROPE_DIM = 64
V_HEAD_DIM = 128
Q_LORA_RANK = 512
KV_LORA_RANK = 256
ROPE_THETA = 10000.0
SSM_GROUP = 16
SSM_STATE = 64
EPS = 1e-6

LANES = 128
HEAD_PAD = 256
SSM_CHUNK = 16
GROUPS_PER_BLOCK = LANES // SSM_GROUP
VMEM_LIMIT = 56 * 1024 * 1024
CAST_BLOCK_BYTES = 8 * 1024 * 1024
NEG_BIG = -1e30

SMALL_W = 1024
OFF_KPE = Q_LORA_RANK + KV_LORA_RANK


def _cparams(*sem):
    return pltpu.CompilerParams(dimension_semantics=sem, vmem_limit_bytes=VMEM_LIMIT)


def _rms(x, w):
    return x * lax.rsqrt(jnp.mean(x * x, axis=-1, keepdims=True) + EPS) * w


def _rope_table_kernel(pos_ref, freq_ref, tab_ref):
    ang = pos_ref[...].astype(F32) * freq_ref[...]
    lane = lax.broadcasted_iota(jnp.int32, ang.shape, 1)
    c = jnp.cos(ang)
    s = jnp.sin(ang)
    tab_ref[...] = jnp.where(lane < 64, c, jnp.where(lane < 96, -s, s))


def _rope_table(positions):
    t = positions.size
    tm = min(t, 2048)
    half = QK_ROPE_DIM // 2
    inv_freq = ROPE_THETA ** (-jnp.arange(0, QK_ROPE_DIM, 2, dtype=F32) / QK_ROPE_DIM)
    freq = jnp.tile(inv_freq, LANES // half).reshape(1, LANES)
    return pl.pallas_call(
        _rope_table_kernel,
        out_shape=jax.ShapeDtypeStruct((t, LANES), F32),
        grid=(t // tm,),
        in_specs=[pl.BlockSpec((tm, 1), lambda i: (i, 0)),
                  pl.BlockSpec((1, LANES), lambda i: (0, 0))],
        out_specs=pl.BlockSpec((tm, LANES), lambda i: (i, 0)),
        compiler_params=_cparams("parallel"),
        name="rope_table",
    )(positions.reshape(t, 1), freq)


def _in_proj_kernel(x_ref, nw_ref, w_ref, b_ref, p_ref, u_ref, h_ref):
    j = pl.program_id(1)

    @pl.when(j == 0)
    def _():
        h_ref[...] = _rms(x_ref[...], nw_ref[...]).astype(BF16)

    acc = jnp.dot(h_ref[...], w_ref[...], preferred_element_type=F32)

    gated = jax.nn.sigmoid(acc + b_ref[...])
    p_ref[...] = jnp.where(j >= 2, gated, acc).astype(BF16)

    @pl.when(j == 0)
    def _():
        u = acc.astype(BF16)
        for g in range(u_ref.shape[0]):
            for c in range(u_ref.shape[1]):
                u_ref[g, c] = u[c * SSM_CHUNK:(c + 1) * SSM_CHUNK, g * LANES:(g + 1) * LANES]


def _in_proj(x, norm_w, w, bias, *, layer, batch, seq):
    t, d = x.shape
    n = w.shape[2]
    tn = SMALL_W
    tm = min(seq, 1024)
    per_b = seq // tm
    cpt = tm // SSM_CHUNK
    nb = tn // LANES
    return pl.pallas_call(
        _in_proj_kernel,
        out_shape=(jax.ShapeDtypeStruct((t, n - tn), BF16),
                   jax.ShapeDtypeStruct((nb, seq // SSM_CHUNK, batch * SSM_CHUNK, LANES), BF16)),
        grid=(t // tm, n // tn),
        in_specs=[pl.BlockSpec((tm, d), lambda i, j: (i, 0)),
                  pl.BlockSpec((1, d), lambda i, j: (0, 0)),
                  pl.BlockSpec((None, d, tn), lambda i, j: (layer, 0, j)),
                  pl.BlockSpec((1, tn), lambda i, j: (0, j))],
        out_specs=(pl.BlockSpec((tm, tn), lambda i, j: (i, jnp.maximum(j - 1, 0))),
                   pl.BlockSpec((nb, cpt, SSM_CHUNK, LANES),
                                lambda i, j: (0, i % per_b, i // per_b, 0))),
        scratch_shapes=[pltpu.VMEM((tm, d), BF16)],
        compiler_params=_cparams("parallel", "arbitrary"),
        name="in_proj",
    )(x, norm_w, w, bias)


def _qkv_up_kernel(p_ref, tab_ref, qn_ref, kvn_ref, wq_ref, wk_ref, wv_ref,
                   q_ref, k_ref, v_ref, *, scale):
    tab = tab_ref[...]
    tabs = tab * scale
    cq = _rms(p_ref[:, 0:Q_LORA_RANK].astype(F32), qn_ref[...]).astype(BF16)
    q_all = jnp.dot(cq, wq_ref[...], preferred_element_type=F32)
    for h in range(MLA_HEADS):
        base = h * HEAD_PAD
        q_ref[h, :, 0:QK_NOPE_DIM] = (q_all[:, base:base + QK_NOPE_DIM] * scale).astype(BF16)
        q_ref[h, :, QK_NOPE_DIM:HEAD_PAD] = (
            q_all[:, base + QK_NOPE_DIM:base + HEAD_PAD] * tabs).astype(BF16)

    ckv = _rms(p_ref[:, Q_LORA_RANK:OFF_KPE].astype(F32), kvn_ref[...]).astype(BF16)
    k_all = jnp.dot(ckv, wk_ref[...], preferred_element_type=F32)
    v_all = jnp.dot(ckv, wv_ref[...], preferred_element_type=F32)
    kt = p_ref[:, OFF_KPE:OFF_KPE + LANES].astype(F32) * tab
    kr = (kt + pltpu.roll(kt, LANES // 2, 1)).astype(BF16)
    ones = jnp.ones((kr.shape[0], LANES), BF16)
    for h in range(MLA_HEADS):
        k_ref[h, :, 0:QK_NOPE_DIM] = k_all[:, h * QK_NOPE_DIM:(h + 1) * QK_NOPE_DIM].astype(BF16)
        k_ref[h, :, QK_NOPE_DIM:HEAD_PAD] = kr
        v_ref[h, :, 0:V_HEAD_DIM] = v_all[:, h * V_HEAD_DIM:(h + 1) * V_HEAD_DIM].astype(BF16)
        v_ref[h, :, V_HEAD_DIM:HEAD_PAD] = ones


def _qkv_up(proj, tab, q_norm, kv_norm, wq, wk, wv, *, layer, batch, seq):
    tm = min(seq, 512)
    nb = seq // tm
    hshape = jax.ShapeDtypeStruct((batch, MLA_HEADS, seq, HEAD_PAD), BF16)
    hspec = pl.BlockSpec((None, MLA_HEADS, tm, HEAD_PAD), lambda b, i: (b, 0, i, 0))
    scale = (QK_NOPE_DIM + QK_ROPE_DIM) ** -0.5 * math.log2(math.e)
    const = lambda b, i: (0, 0)
    return pl.pallas_call(
        functools.partial(_qkv_up_kernel, scale=scale),
        out_shape=(hshape, hshape, hshape),
        grid=(batch, nb),
        in_specs=[pl.BlockSpec((tm, SMALL_W), lambda b, i: (b * nb + i, 0)),
                  pl.BlockSpec((tm, LANES), lambda b, i: (b * nb + i, 0)),
                  pl.BlockSpec((1, Q_LORA_RANK), const),
                  pl.BlockSpec((1, KV_LORA_RANK), const),
                  pl.BlockSpec((None,) + wq.shape[1:], lambda b, i: (layer, 0, 0)),
                  pl.BlockSpec((None,) + wk.shape[1:], lambda b, i: (layer, 0, 0)),
                  pl.BlockSpec((None,) + wv.shape[1:], lambda b, i: (layer, 0, 0))],
        out_specs=(hspec, hspec, hspec),
        compiler_params=_cparams("parallel", "parallel"),
        name="qkv_up",
    )(proj, tab, q_norm, kv_norm, wq, wk, wv)


def _attn_kernel(q_ref, k_ref, v_ref, o_ref, *, tq):
    seq = q_ref.shape[0]
    per = tq // LANES
    row_chunk = lax.broadcasted_iota(jnp.int32, (tq, LANES), 0) // CHUNK
    lane = lax.broadcasted_iota(jnp.int32, (tq, LANES), 1)
    masks = [((c * LANES + lane) // CHUNK) <= row_chunk for c in range(per)]
    nq = seq // tq
    order = [t for pair in zip(range(nq - 1, -1, -1), range(nq)) for t in pair][:nq]
    for i in order:
        kv = (i + 1) * tq
        q = q_ref[i * tq:(i + 1) * tq, :]
        s = lax.dot_general(q, k_ref[0:kv, :], (((1,), (1,)), ((), ())),
                            preferred_element_type=F32)
        pieces = []
        for c in range(kv // LANES):
            piece = s[:, c * LANES:(c + 1) * LANES]
            if c >= i * per:
                piece = jnp.where(masks[c - i * per], piece, NEG_BIG)
            pieces.append(piece)
        mx = pieces[0]
        for piece in pieces[1:]:
            mx = jnp.maximum(mx, piece)
        m = jnp.broadcast_to(jnp.max(mx, axis=1, keepdims=True), (tq, LANES))
        p = jnp.concatenate([jnp.exp2(piece - m) for piece in pieces], axis=1).astype(BF16)
        acc = jnp.dot(p, v_ref[0:kv, :], preferred_element_type=F32)
        o_ref[i * tq:(i + 1) * tq, :] = (
            acc[:, 0:V_HEAD_DIM] / acc[:, V_HEAD_DIM:HEAD_PAD]).astype(o_ref.dtype)


def _attention(q, k, v):
    batch, heads, seq, _ = q.shape
    tq = min(seq, 512)
    spec = pl.BlockSpec((None, None, seq, HEAD_PAD), lambda b, h: (b, h, 0, 0))
    return pl.pallas_call(
        functools.partial(_attn_kernel, tq=tq),
        out_shape=jax.ShapeDtypeStruct((batch, seq, heads * V_HEAD_DIM), BF16),
        grid=(batch, heads),
        in_specs=[spec, spec, spec],
        out_specs=pl.BlockSpec((None, seq, V_HEAD_DIM), lambda b, h: (b, 0, h)),
        compiler_params=_cparams("parallel", "parallel"),
        name="attn",
    )(q, k, v)


def _ssm_weights(a_re, a_im, log_dt, b_re, b_im, c_re, c_im, d_skip):
    L = SSM_CHUNK
    g, n = a_re.shape
    gpb = GROUPS_PER_BLOCK
    nb = g // gpb
    dt = jnp.exp(log_dt)[:, None]
    mag = jnp.exp(a_re * dt)
    abar_re = mag * jnp.cos(a_im * dt)
    abar_im = mag * jnp.sin(a_im * dt)
    den = a_re * a_re + a_im * a_im
    nr = abar_re - 1.0
    f_re = (nr * a_re + abar_im * a_im) / den
    f_im = (abar_im * a_re - nr * a_im) / den
    bb_re = f_re[..., None] * b_re - f_im[..., None] * b_im
    bb_im = f_re[..., None] * b_im + f_im[..., None] * b_re
    kk = jnp.arange(L + 1, dtype=F32)[:, None, None]
    pmag = jnp.exp(kk * (a_re * dt))
    pw_re = pmag * jnp.cos(kk * (a_im * dt))
    pw_im = pmag * jnp.sin(kk * (a_im * dt))
    eye = jnp.eye(gpb, dtype=F32)

    def block_diag(a):
        lead = a.shape[:-3]
        r, c = a.shape[-2:]
        a = a.reshape(lead + (nb, gpb, r, 1, c)) * eye[:, None, :, None]
        return a.reshape(lead + (nb, gpb * r, gpb * c))

    bb_bd =(block_diag(bb_re.transpose(0, 2, 1)), block_diag(bb_im.transpose(0, 2, 1)))
    c_bd = (block_diag(c_re.transpose(0, 2, 1)), block_diag(c_im.transpose(0, 2, 1)))
    pw = jnp.concatenate([pw_re.reshape(L + 1, nb, gpb * n), pw_im.reshape(L + 1, nb, gpb * n)],
                         axis=-1).transpose(1, 0, 2)
    pw_col = jnp.concatenate([pw_re.reshape(L + 1, nb, gpb * n), pw_im.reshape(L + 1, nb, gpb * n)],
                             axis=0).transpose(1, 2, 0)
    d_row = jnp.tile(d_skip.reshape(nb, 1, LANES), (1, 1, L))
    return bb_bd, c_bd, pw, pw_col, d_row


def _steps_to_lanes(u_ref, stage_ref):
    stage_ref[...] = u_ref[...].astype(F32)
    rows = stage_ref.shape[0] // SSM_CHUNK
    return jnp.concatenate([stage_ref[pl.ds(i, rows, stride=SSM_CHUNK), :]
                            for i in range(SSM_CHUNK)], axis=1)


def _ssm_in_kernel(u_ref, bbr_ref, bbi_ref, pw_ref, v_ref, w_ref, stage_ref):
    @pl.when(pl.program_id(1) == 0)
    def _():
        half = bbr_ref.shape[-1]
        bbr = bbr_ref[...]
        bbi = bbi_ref[...]
        for i in range(SSM_CHUNK):
            k = SSM_CHUNK - 1 - i
            pr = pw_ref[k:k + 1, 0:half]
            pi = pw_ref[k:k + 1, half:]
            w_ref[i * LANES:(i + 1) * LANES, 0:half] = (pr * bbr - pi * bbi).astype(BF16)
            w_ref[i * LANES:(i + 1) * LANES, half:] = (pr * bbi + pi * bbr).astype(BF16)

    u = _steps_to_lanes(u_ref, stage_ref).astype(BF16)
    v_ref[...] = jnp.dot(u, w_ref[...], preferred_element_type=F32)


def _ssm_scan_kernel(v_ref, a_ref, h_ref, *, batch, chunks):
    half = v_ref.shape[-1] // 2
    ar = jnp.broadcast_to(a_ref[:, 0:half], (batch, half))
    ai = jnp.broadcast_to(a_ref[:, half:], (batch, half))

    def step(c, carry):
        hr, hi = carry
        rows = pl.ds(pl.multiple_of(c * batch, batch), batch)
        h_ref[rows, 0:half] = hr
        h_ref[rows, half:] = hi
        return (ar * hr - ai * hi + v_ref[rows, 0:half],
                ar * hi + ai * hr + v_ref[rows, half:])

    zero = jnp.zeros((batch, half), F32)
    lax.fori_loop(0, chunks, step, (zero, zero))


def _ssm_out_kernel(u_ref, h_ref, bbr_ref, bbi_ref, cr_ref, ci_ref, pw_ref, pc_ref, d_ref, o_ref,
                    t_ref, z_ref, stage_ref):
    @pl.when(pl.program_id(1) == 0)
    def _():
        L = SSM_CHUNK
        half = cr_ref.shape[0]
        bbr = bbr_ref[...]
        bbi = bbi_ref[...]
        cr = cr_ref[...]
        ci = ci_ref[...]
        ab = []
        for tau in range(L):
            pr = pw_ref[tau:tau + 1, 0:half]
            pi = pw_ref[tau:tau + 1, half:]
            ab.append(jnp.concatenate([pr * bbr - pi * bbi, pr * bbi + pi * bbr], axis=1))
        kern = jnp.dot(jnp.concatenate(ab, axis=0), jnp.concatenate([cr, -ci], axis=0),
                       precision=lax.Precision.HIGHEST, preferred_element_type=F32)
        lag = [kern[tau * LANES:(tau + 1) * LANES, :].astype(BF16) for tau in range(L)]
        zero = jnp.zeros((LANES, LANES), BF16)
        for i in range(L):
            for j in range(L):
                t_ref[i * LANES:(i + 1) * LANES, j * LANES:(j + 1) * LANES] = (
                    lag[j - i] if j >= i else zero)
        for j in range(L):
            pr = pc_ref[:, j + 1:j + 2]
            pi = pc_ref[:, L + j + 2:L + j + 3]
            z_ref[0:half, j * LANES:(j + 1) * LANES] = (cr * pr - ci * pi).astype(BF16)
            z_ref[half:, j * LANES:(j + 1) * LANES] = (-(cr * pi + ci * pr)).astype(BF16)

    u = _steps_to_lanes(u_ref, stage_ref)
    ub = u.astype(BF16)
    mid = t_ref.shape[0] // 2
    toep = jnp.concatenate(
        [jnp.dot(ub[:, 0:mid], t_ref[0:mid, 0:mid], preferred_element_type=F32),
         jnp.dot(ub, t_ref[:, mid:], preferred_element_type=F32)], axis=1)
    y = (toep
         + jnp.dot(h_ref[...].astype(BF16), z_ref[...], preferred_element_type=F32)
         + u * d_ref[...])
    z = jax.nn.gelu(y)
    rows = z.shape[0]
    for i in range(SSM_CHUNK):
        stage_ref[pl.ds(i, rows, stride=SSM_CHUNK), :] = z[:, i * LANES:(i + 1) * LANES]
    o_ref[...] = stage_ref[...].astype(o_ref.dtype)


def _ssm(u, weights, *, batch):
    (bb_r, bb_i), (c_r, c_i), pw, pw_col, d_row = weights
    nb, tokens, _ = u.shape
    m = tokens // SSM_CHUNK
    width = SSM_CHUNK * LANES
    states = pw.shape[-1]
    tr = min(m, 512)
    chunks = m // batch
    grow = lambda g, r: (g, r, 0)
    u_spec = pl.BlockSpec((None, tr * SSM_CHUNK, LANES), grow)
    stage = pltpu.VMEM((tr * SSM_CHUNK, LANES), F32)

    def whole(a):
        return pl.BlockSpec((None,) + a.shape[1:], lambda g, r: (g,) + (0,) * (a.ndim - 1))

    v = pl.pallas_call(
        _ssm_in_kernel,
        out_shape=jax.ShapeDtypeStruct((nb, m, states), F32),
        grid=(nb, m // tr),
        in_specs=[u_spec, whole(bb_r), whole(bb_i), whole(pw)],
        out_specs=pl.BlockSpec((None, tr, states), grow),
        scratch_shapes=[pltpu.VMEM((width, states), BF16), stage],
        compiler_params=_cparams("parallel", "arbitrary"),
        name="ssm_in",
    )(u, bb_r, bb_i, pw)

    h = pl.pallas_call(
        functools.partial(_ssm_scan_kernel, batch=batch, chunks=chunks),
        out_shape=jax.ShapeDtypeStruct((nb, m, states), F32),
        grid=(nb,),
        in_specs=[pl.BlockSpec((None, m, states), lambda g: (g, 0, 0)),
                  pl.BlockSpec((None, 1, states), lambda g: (g, 0, 0))],
        out_specs=pl.BlockSpec((None, m, states), lambda g: (g, 0, 0)),
        compiler_params=_cparams("parallel"),
        name="ssm_scan",
    )(v, pw[:, SSM_CHUNK:SSM_CHUNK + 1, :])

    return pl.pallas_call(
        _ssm_out_kernel,
        out_shape=jax.ShapeDtypeStruct(u.shape, BF16),
        grid=(nb, m // tr),
        in_specs=[u_spec,
                  pl.BlockSpec((None, tr, states), grow),
                  whole(bb_r), whole(bb_i), whole(c_r), whole(c_i), whole(pw), whole(pw_col),
                  whole(d_row)],
        out_specs=u_spec,
        scratch_shapes=[pltpu.VMEM((width, width), BF16), pltpu.VMEM((states, width), BF16),
                        stage],
        compiler_params=_cparams("parallel", "arbitrary"),
        name="ssm_out",
    )(u, h, bb_r, bb_i, c_r, c_i, pw, pw_col, d_row)


def _merge_kernel(o_ref, z_ref, ga_ref, gb_ref, wo_ref, w1_ref, w2_ref, b1_ref, b2_ref, m_ref):
    tm = o_ref.shape[0]
    z = jnp.concatenate([z_ref[g].reshape(tm, LANES) for g in range(z_ref.shape[0])], axis=1)
    z1 = jnp.dot(z, w1_ref[...], preferred_element_type=F32) + b1_ref[...]
    z2 = jnp.dot(z, w2_ref[...], preferred_element_type=F32) + b2_ref[...]
    s = z1 * jax.nn.sigmoid(z2)
    a = jnp.dot(o_ref[...], wo_ref[...], preferred_element_type=F32)
    m_ref[...] = (ga_ref[...].astype(F32) * a + gb_ref[...].astype(F32) * s).astype(m_ref.dtype)


def _merge(o, z_cb, proj, w_o, w_glu, b_glu, *, layer, seq):
    t, d = o.shape
    nb = z_cb.shape[0]
    zw = nb * LANES
    tm = min(seq, 1024)
    per_b = seq // tm
    tn = 512
    nc = d // tn
    ga0 = SMALL_W // tn
    return pl.pallas_call(
        _merge_kernel,
        out_shape=jax.ShapeDtypeStruct((t, d), BF16),
        grid=(t // tm, nc),
        in_specs=[pl.BlockSpec((tm, d), lambda i, j: (i, 0)),
                  pl.BlockSpec((nb, tm // SSM_CHUNK, SSM_CHUNK, LANES),
                               lambda i, j: (0, i % per_b, i // per_b, 0)),
                  pl.BlockSpec((tm, tn), lambda i, j: (i, ga0 + j)),
                  pl.BlockSpec((tm, tn), lambda i, j: (i, ga0 + nc + j)),
                  pl.BlockSpec((None, d, tn), lambda i, j: (layer, 0, j)),
                  pl.BlockSpec((None, zw, tn), lambda i, j: (layer, 0, j)),
                  pl.BlockSpec((None, zw, tn), lambda i, j: (layer, 0, nc + j)),
                  pl.BlockSpec((1, tn), lambda i, j: (0, j)),
                  pl.BlockSpec((1, tn), lambda i, j: (0, nc + j))],
        out_specs=pl.BlockSpec((tm, tn), lambda i, j: (i, j)),
        compiler_params=_cparams("parallel", "arbitrary"),
        name="merge",
    )(o, z_cb, proj, proj, w_o, w_glu, w_glu, b_glu, b_glu)


def _out_proj_kernel(m_ref, x_ref, w_ref, nw_ref, o_ref):
    mix = jnp.dot(m_ref[...], w_ref[...], preferred_element_type=F32)
    o_ref[...] = x_ref[...] + _rms(mix, nw_ref[...])


def _out_proj(merged, x, w_out, norm_w, *, layer):
    t, d = x.shape
    tm = min(t, 512)
    return pl.pallas_call(
        _out_proj_kernel,
        out_shape=jax.ShapeDtypeStruct((t, d), F32),
        grid=(t // tm,),
        in_specs=[pl.BlockSpec((tm, d), lambda i: (i, 0)),
                  pl.BlockSpec((tm, d), lambda i: (i, 0)),
                  pl.BlockSpec((None, d, d), lambda i: (layer, 0, 0)),
                  pl.BlockSpec((1, d), lambda i: (0, 0))],
        out_specs=pl.BlockSpec((tm, d), lambda i: (i, 0)),
        compiler_params=_cparams("parallel"),
        name="out_proj",
    )(merged, x, w_out, norm_w)


def _ffn_kernel(x_ref, nw_ref, wg_ref, wu_ref, wd_ref, pw_ref, o_ref, h_ref, acc_ref):
    j = pl.program_id(1)

    @pl.when(j == 0)
    def _():
        h_ref[...] = _rms(x_ref[...], nw_ref[...]).astype(BF16)
        acc_ref[...] = jnp.zeros(acc_ref.shape, F32)

    h = h_ref[...]
    gate = jnp.dot(h, wg_ref[...], preferred_element_type=F32)
    up = jnp.dot(h, wu_ref[...], preferred_element_type=F32)
    act = (jax.nn.silu(gate) * up).astype(BF16)
    acc_ref[...] += jnp.dot(act, wd_ref[...], preferred_element_type=F32)

    @pl.when(j == pl.num_programs(1) - 1)
    def _():
        o_ref[...] = x_ref[...] + _rms(acc_ref[...], pw_ref[...])


def _ffn(x, pre_w, w_gate, w_up, w_down, post_w, *, layer):
    t, d = x.shape
    f = w_gate.shape[2]
    tm = min(t, 512)
    tf = 512
    return pl.pallas_call(
        _ffn_kernel,
        out_shape=jax.ShapeDtypeStruct((t, d), F32),
        grid=(t // tm, f // tf),
        in_specs=[pl.BlockSpec((tm, d), lambda i, j: (i, 0)),
                  pl.BlockSpec((1, d), lambda i, j: (0, 0)),
                  pl.BlockSpec((None, d, tf), lambda i, j: (layer, 0, j)),
                  pl.BlockSpec((None, d, tf), lambda i, j: (layer, 0, j)),
                  pl.BlockSpec((None, tf, d), lambda i, j: (layer, j, 0)),
                  pl.BlockSpec((1, d), lambda i, j: (0, 0))],
        out_specs=pl.BlockSpec((tm, d), lambda i, j: (i, 0)),
        scratch_shapes=[pltpu.VMEM((tm, d), BF16), pltpu.VMEM((tm, d), F32)],
        compiler_params=_cparams("parallel", "arbitrary"),
        name="ffn",
    )(x, pre_w, w_gate, w_up, w_down, post_w)


def _cast_kernel(x_ref, o_ref):
    o_ref[...] = x_ref[...].astype(o_ref.dtype)


def _to_bf16(w):
    layers, r, c = w.shape
    tr = r
    while tr * c * 4 > CAST_BLOCK_BYTES and tr % 32 == 0:
        tr //= 2
    spec = pl.BlockSpec((None, tr, c), lambda l, i: (l, i, 0))
    return pl.pallas_call(
        _cast_kernel,
        out_shape=jax.ShapeDtypeStruct(w.shape, BF16),
        grid=(layers, r // tr),
        in_specs=[spec],
        out_specs=spec,
        compiler_params=_cparams("parallel", "parallel"),
        name="to_bf16",
    )(w)


def _regroup_call(body, w, out_cols, name):
    layers, r, c = w.shape
    tr = r
    while tr * c * 4 > CAST_BLOCK_BYTES and tr % 32 == 0:
        tr //= 2
    outs = pl.pallas_call(
        body,
        out_shape=[jax.ShapeDtypeStruct((layers, r, n), BF16) for n in out_cols],
        grid=(layers, r // tr),
        in_specs=[pl.BlockSpec((None, tr, c), lambda l, i: (l, i, 0))],
        out_specs=[pl.BlockSpec((None, tr, n), lambda l, i: (l, i, 0)) for n in out_cols],
        compiler_params=_cparams("parallel", "parallel"),
        name=name,
    )(w)
    return outs


def _w_in_body(x_ref, o_ref):
    off_ssm = OFF_KPE + QK_ROPE_DIM
    half = QK_ROPE_DIM // 2
    gates = o_ref.shape[-1] - 2 * SMALL_W
    off_gate = x_ref.shape[-1] - gates
    ssm_w = off_gate - off_ssm
    x = x_ref[...].astype(BF16)
    o_ref[:, 0:ssm_w] = x[:, off_ssm:off_gate]
    o_ref[:, ssm_w:ssm_w + off_ssm] = x[:, 0:off_ssm]
    base = ssm_w + off_ssm
    o_ref[:, base:base + half] = x[:, OFF_KPE + half:off_ssm]
    o_ref[:, base + half:base + 2 * half] = x[:, OFF_KPE:OFF_KPE + half]
    o_ref[:, base + 2 * half:2 * SMALL_W] = jnp.zeros(
        (x.shape[0], 2 * SMALL_W - base - 2 * half), BF16)
    o_ref[:, 2 * SMALL_W:] = x[:, off_gate:]


def _w_uq_body(x_ref, o_ref):
    x = x_ref[...].astype(BF16)
    qk = QK_NOPE_DIM + QK_ROPE_DIM
    half = QK_ROPE_DIM // 2
    for h in range(MLA_HEADS):
        src = h * qk
        dst = h * HEAD_PAD
        o_ref[:, dst:dst + qk] = x[:, src:src + qk]
        o_ref[:, dst + qk:dst + qk + half] = x[:, src + QK_NOPE_DIM + half:src + qk]
        o_ref[:, dst + qk + half:dst + HEAD_PAD] = x[:, src + QK_NOPE_DIM:src + QK_NOPE_DIM + half]


def _w_ukv_body(x_ref, k_ref, v_ref):
    x = x_ref[...].astype(BF16)
    per = QK_NOPE_DIM + V_HEAD_DIM
    for h in range(MLA_HEADS):
        k_ref[:, h * QK_NOPE_DIM:(h + 1) * QK_NOPE_DIM] = x[:, h * per:h * per + QK_NOPE_DIM]
        v_ref[:, h * V_HEAD_DIM:(h + 1) * V_HEAD_DIM] = x[:, h * per + QK_NOPE_DIM:(h + 1) * per]


def _regroup_w_in(w_in):
    off_ssm = OFF_KPE + QK_ROPE_DIM
    return _regroup_call(_w_in_body, w_in, [w_in.shape[-1] - off_ssm + SMALL_W], "prep_w_in")[0]


def _regroup_w_uq(w_uq):
    return _regroup_call(_w_uq_body, w_uq, [MLA_HEADS * HEAD_PAD], "prep_w_uq")[0]


def _regroup_w_ukv(w_ukv):
    return _regroup_call(_w_ukv_body, w_ukv,
                         [MLA_HEADS * QK_NOPE_DIM, MLA_HEADS * V_HEAD_DIM], "prep_w_ukv")


def kernel(x, positions, pre_mix_norm, w_in, b_gate, q_norm, kv_norm, w_uq, w_ukv, w_o_mla,
           ssm_a_re, ssm_a_im, ssm_log_dt, ssm_b_re, ssm_b_im, ssm_c_re, ssm_c_im, ssm_d,
           w_glu, b_glu, w_out, post_mix_norm, pre_ffn_norm, w_ffn_gate, w_ffn_up,
           w_ffn_down, post_ffn_norm):
    batch, seq, d = x.shape
    depth = w_in.shape[0]
    t = batch * seq
    ssm_w = ssm_a_re.shape[1] * SSM_GROUP
    chunks = seq // SSM_CHUNK
    assert seq % max(CHUNK, SSM_CHUNK) == 0 and batch % 8 == 0
    assert ssm_w == SMALL_W and d % SMALL_W == 0

    w_in_p = _regroup_w_in(w_in)
    bias = jnp.concatenate([jnp.zeros((depth, w_in_p.shape[-1] - b_gate.shape[-1]), F32), b_gate],
                           axis=-1)
    wq = _regroup_w_uq(w_uq)
    wk, wv = _regroup_w_ukv(w_ukv)
    w_o_b, w_glu_b, w_out_b = _to_bf16(w_o_mla), _to_bf16(w_glu), _to_bf16(w_out)
    w_fg, w_fu, w_fd = _to_bf16(w_ffn_gate), _to_bf16(w_ffn_up), _to_bf16(w_ffn_down)
    ssm_weights = jax.vmap(_ssm_weights)(ssm_a_re, ssm_a_im, ssm_log_dt, ssm_b_re, ssm_b_im,
                                         ssm_c_re, ssm_c_im, ssm_d)

    tab = _rope_table(positions)
    xf = x.reshape(t, d)
    row = lambda v: v.reshape(1, -1)
    for l in range(depth):
        proj, u_cb = _in_proj(xf, row(pre_mix_norm[l]), w_in_p, row(bias[l]),
                              layer=l, batch=batch, seq=seq)
        q, k, v = _qkv_up(proj, tab, row(q_norm[l]), row(kv_norm[l]), wq, wk, wv,
                          layer=l, batch=batch, seq=seq)
        o = _attention(q, k, v).reshape(t, MLA_HEADS * V_HEAD_DIM)

        weights = jax.tree.map(lambda a: a[l], ssm_weights)
        nb = u_cb.shape[0]
        z_cb = _ssm(u_cb.reshape(nb, chunks * batch * SSM_CHUNK, LANES), weights, batch=batch)
        z_cb = z_cb.reshape(u_cb.shape)

        merged = _merge(o, z_cb, proj, w_o_b, w_glu_b, row(b_glu[l]), layer=l, seq=seq)
        xf = _out_proj(merged, xf, w_out_b, row(post_mix_norm[l]), layer=l)
        xf = _ffn(xf, row(pre_ffn_norm[l]), w_fg, w_fu, w_fd, row(post_ffn_norm[l]), layer=l)
    return xf.reshape(batch, seq, d)
```

```python
import functools
import math

import jax
import jax.numpy as jnp
from jax import lax
from jax.experimental import pallas as pl
from jax.experimental.pallas import tpu as pltpu

F32 = jnp.float32
BF16 = jnp.bfloat16

CHUNK = 64
MLA_HEADS = 16
QK_NOPE_DIM = 128
QK_ROPE_DIM = 64
V_HEAD_DIM = 128
Q_LORA_RANK = 512
KV_LORA_RANK = 256
ROPE_THETA = 10000.0
SSM_GROUP = 16
SSM_STATE = 64
EPS = 1e-6

LANES = 128
HEAD_PAD = 256
SSM_CHUNK = 16
GROUPS_PER_BLOCK = LANES // SSM_GROUP
VMEM_LIMIT = 56 * 1024 * 1024
CAST_BLOCK_BYTES = 8 * 1024 * 1024
NEG_BIG = -1e30

SMALL_W = 1024
OFF_KPE = Q_LORA_RANK + KV_LORA_RANK


def _cparams(*sem):
    return pltpu.CompilerParams(dimension_semantics=sem, vmem_limit_bytes=VMEM_LIMIT)


def _rms(x, w):
    return x * lax.rsqrt(jnp.mean(x * x, axis=-1, keepdims=True) + EPS) * w


def _rope_table_kernel(pos_ref, freq_ref, tab_ref):
    ang = pos_ref[...].astype(F32) * freq_ref[...]
    lane = lax.broadcasted_iota(jnp.int32, ang.shape, 1)
    c = jnp.cos(ang)
    s = jnp.sin(ang)
    tab_ref[...] = jnp.where(lane < 64, c, jnp.where(lane < 96, -s, s))


def _rope_table(positions):
    t = positions.size
    tm = min(t, 2048)
    half = QK_ROPE_DIM // 2
    inv_freq = ROPE_THETA ** (-jnp.arange(0, QK_ROPE_DIM, 2, dtype=F32) / QK_ROPE_DIM)
    freq = jnp.tile(inv_freq, LANES // half).reshape(1, LANES)
    return pl.pallas_call(
        _rope_table_kernel,
        out_shape=jax.ShapeDtypeStruct((t, LANES), F32),
        grid=(t // tm,),
        in_specs=[pl.BlockSpec((tm, 1), lambda i: (i, 0)),
                  pl.BlockSpec((1, LANES), lambda i: (0, 0))],
        out_specs=pl.BlockSpec((tm, LANES), lambda i: (i, 0)),
        compiler_params=_cparams("parallel"),
        name="rope_table",
    )(positions.reshape(t, 1), freq)


def _in_proj_kernel(x_ref, nw_ref, w_ref, b_ref, p_ref, u_ref, h_ref):
    j = pl.program_id(1)

    @pl.when(j == 0)
    def _():
        h_ref[...] = _rms(x_ref[...], nw_ref[...]).astype(BF16)

    acc = jnp.dot(h_ref[...], w_ref[...], preferred_element_type=F32)

    gated = jax.nn.sigmoid(acc + b_ref[...])
    p_ref[...] = jnp.where(j >= 2, gated, acc).astype(BF16)

    @pl.when(j == 0)
    def _():
        u = acc.astype(BF16)
        for g in range(u_ref.shape[0]):
            for c in range(u_ref.shape[1]):
                u_ref[g, c] = u[c * SSM_CHUNK:(c + 1) * SSM_CHUNK, g * LANES:(g + 1) * LANES]


def _in_proj(x, norm_w, w, bias, *, layer, batch, seq):
    t, d = x.shape
    n = w.shape[2]
    tn = SMALL_W
    tm = min(seq, 1024)
    per_b = seq // tm
    cpt = tm // SSM_CHUNK
    nb = tn // LANES
    return pl.pallas_call(
        _in_proj_kernel,
        out_shape=(jax.ShapeDtypeStruct((t, n - tn), BF16),
                   jax.ShapeDtypeStruct((nb, seq // SSM_CHUNK, batch * SSM_CHUNK, LANES), BF16)),
        grid=(t // tm, n // tn),
        in_specs=[pl.BlockSpec((tm, d), lambda i, j: (i, 0)),
                  pl.BlockSpec((1, d), lambda i, j: (0, 0)),
                  pl.BlockSpec((None, d, tn), lambda i, j: (layer, 0, j)),
                  pl.BlockSpec((1, tn), lambda i, j: (0, j))],
        out_specs=(pl.BlockSpec((tm, tn), lambda i, j: (i, jnp.maximum(j - 1, 0))),
                   pl.BlockSpec((nb, cpt, SSM_CHUNK, LANES),
                                lambda i, j: (0, i % per_b, i // per_b, 0))),
        scratch_shapes=[pltpu.VMEM((tm, d), BF16)],
        compiler_params=_cparams("parallel", "arbitrary"),
        name="in_proj",
    )(x, norm_w, w, bias)


def _qkv_up_kernel(p_ref, tab_ref, qn_ref, kvn_ref, wq_ref, wk_ref, wv_ref,
                   q_ref, k_ref, v_ref, *, scale):
    tab = tab_ref[...]
    tabs = tab * scale
    cq = _rms(p_ref[:, 0:Q_LORA_RANK].astype(F32), qn_ref[...]).astype(BF16)
    q_all = jnp.dot(cq, wq_ref[...], preferred_element_type=F32)
    for h in range(MLA_HEADS):
        base = h * HEAD_PAD
        q_ref[h, :, 0:QK_NOPE_DIM] = (q_all[:, base:base + QK_NOPE_DIM] * scale).astype(BF16)
        q_ref[h, :, QK_NOPE_DIM:HEAD_PAD] = (
            q_all[:, base + QK_NOPE_DIM:base + HEAD_PAD] * tabs).astype(BF16)

    ckv = _rms(p_ref[:, Q_LORA_RANK:OFF_KPE].astype(F32), kvn_ref[...]).astype(BF16)
    k_all = jnp.dot(ckv, wk_ref[...], preferred_element_type=F32)
    v_all = jnp.dot(ckv, wv_ref[...], preferred_element_type=F32)
    kt = p_ref[:, OFF_KPE:OFF_KPE + LANES].astype(F32) * tab
    kr = (kt + pltpu.roll(kt, LANES // 2, 1)).astype(BF16)
    for h in range(MLA_HEADS):
        k_ref[h, :, 0:QK_NOPE_DIM] = k_all[:, h * QK_NOPE_DIM:(h + 1) * QK_NOPE_DIM].astype(BF16)
        k_ref[h, :, QK_NOPE_DIM:HEAD_PAD] = kr
        v_ref[h] = v_all[:, h * V_HEAD_DIM:(h + 1) * V_HEAD_DIM].astype(BF16)


def _qkv_up(proj, tab, q_norm, kv_norm, wq, wk, wv, *, layer, batch, seq):
    tm = min(seq, 512)
    nb = seq // tm
    def per_head(width):
        return (jax.ShapeDtypeStruct((batch, MLA_HEADS, seq, width), BF16),
                pl.BlockSpec((None, MLA_HEADS, tm, width), lambda b, i: (b, 0, i, 0)))

    (q_shape, q_spec), (k_shape, k_spec), (v_shape, v_spec) = (
        per_head(HEAD_PAD), per_head(HEAD_PAD), per_head(V_HEAD_DIM))
    scale = (QK_NOPE_DIM + QK_ROPE_DIM) ** -0.5 * math.log2(math.e)
    const = lambda b, i: (0, 0)
    return pl.pallas_call(
        functools.partial(_qkv_up_kernel, scale=scale),
        out_shape=(q_shape, k_shape, v_shape),
        grid=(batch, nb),
        in_specs=[pl.BlockSpec((tm, SMALL_W), lambda b, i: (b * nb + i, 0)),
                  pl.BlockSpec((tm, LANES), lambda b, i: (b * nb + i, 0)),
                  pl.BlockSpec((1, Q_LORA_RANK), const),
                  pl.BlockSpec((1, KV_LORA_RANK), const),
                  pl.BlockSpec((None,) + wq.shape[1:], lambda b, i: (layer, 0, 0)),
                  pl.BlockSpec((None,) + wk.shape[1:], lambda b, i: (layer, 0, 0)),
                  pl.BlockSpec((None,) + wv.shape[1:], lambda b, i: (layer, 0, 0))],
        out_specs=(q_spec, k_spec, v_spec),
        compiler_params=_cparams("parallel", "parallel"),
        name="qkv_up",
    )(proj, tab, q_norm, kv_norm, wq, wk, wv)


def _attn_kernel(q_ref, k_ref, vh_ref, o_ref, v_ref, *, tq):
    seq = q_ref.shape[0]
    v_ref[:, 0:V_HEAD_DIM] = vh_ref[...]
    v_ref[:, V_HEAD_DIM:HEAD_PAD] = jnp.ones((seq, HEAD_PAD - V_HEAD_DIM), BF16)
    per = tq // LANES
    row_chunk = lax.broadcasted_iota(jnp.int32, (tq, LANES), 0) // CHUNK
    lane = lax.broadcasted_iota(jnp.int32, (tq, LANES), 1)
    masks = [((c * LANES + lane) // CHUNK) <= row_chunk for c in range(per)]
    nq = seq // tq
    order = [t for pair in zip(range(nq - 1, -1, -1), range(nq)) for t in pair][:nq]
    for i in order:
        kv = (i + 1) * tq
        q = q_ref[i * tq:(i + 1) * tq, :]
        s = lax.dot_general(q, k_ref[0:kv, :], (((1,), (1,)), ((), ())),
                            preferred_element_type=F32)
        pieces = []
        for c in range(kv // LANES):
            piece = s[:, c * LANES:(c + 1) * LANES]
            if c >= i * per:
                piece = jnp.where(masks[c - i * per], piece, NEG_BIG)
            pieces.append(piece)
        mx = pieces[0]
        for piece in pieces[1:]:
            mx = jnp.maximum(mx, piece)
        m = jnp.broadcast_to(jnp.max(mx, axis=1, keepdims=True), (tq, LANES))
        p = jnp.concatenate([jnp.exp2(piece - m) for piece in pieces], axis=1).astype(BF16)
        acc = jnp.dot(p, v_ref[0:kv, :], preferred_element_type=F32)
        o_ref[i * tq:(i + 1) * tq, :] = (
            acc[:, 0:V_HEAD_DIM] / acc[:, V_HEAD_DIM:HEAD_PAD]).astype(o_ref.dtype)


def _attention(q, k, v):
    batch, heads, seq, _ = q.shape
    tq = min(seq, 512)

    def per_head(width):
        return pl.BlockSpec((None, None, seq, width), lambda b, h: (b, h, 0, 0))

    return pl.pallas_call(
        functools.partial(_attn_kernel, tq=tq),
        out_shape=jax.ShapeDtypeStruct((batch, seq, heads * V_HEAD_DIM), BF16),
        grid=(batch, heads),
        in_specs=[per_head(HEAD_PAD), per_head(HEAD_PAD), per_head(V_HEAD_DIM)],
        out_specs=pl.BlockSpec((None, seq, V_HEAD_DIM), lambda b, h: (b, 0, h)),
        scratch_shapes=[pltpu.VMEM((seq, HEAD_PAD), BF16)],
        compiler_params=_cparams("parallel", "parallel"),
        name="attn",
    )(q, k, v)


def _ssm_weights(a_re, a_im, log_dt, b_re, b_im, c_re, c_im, d_skip):
    L = SSM_CHUNK
    g, n = a_re.shape
    gpb = GROUPS_PER_BLOCK
    nb = g // gpb
    dt = jnp.exp(log_dt)[:, None]
    mag = jnp.exp(a_re * dt)
    abar_re = mag * jnp.cos(a_im * dt)
    abar_im = mag * jnp.sin(a_im * dt)
    den = a_re * a_re + a_im * a_im
    nr = abar_re - 1.0
    f_re = (nr * a_re + abar_im * a_im) / den
    f_im = (abar_im * a_re - nr * a_im) / den
    bb_re = f_re[..., None] * b_re - f_im[..., None] * b_im
    bb_im = f_re[..., None] * b_im + f_im[..., None] * b_re
    kk = jnp.arange(L + 1, dtype=F32)[:, None, None]
    pmag = jnp.exp(kk * (a_re * dt))
    pw_re = pmag * jnp.cos(kk * (a_im * dt))
    pw_im = pmag * jnp.sin(kk * (a_im * dt))
    eye = jnp.eye(gpb, dtype=F32)

    def block_diag(a):
        lead = a.shape[:-3]
        r, c = a.shape[-2:]
        a = a.reshape(lead + (nb, gpb, r, 1, c)) * eye[:, None, :, None]
        return a.reshape(lead + (nb, gpb * r, gpb * c))

    bb_bd =(block_diag(bb_re.transpose(0, 2, 1)), block_diag(bb_im.transpose(0, 2, 1)))
    c_bd = (block_diag(c_re.transpose(0, 2, 1)), block_diag(c_im.transpose(0, 2, 1)))
    pw = jnp.concatenate([pw_re.reshape(L + 1, nb, gpb * n), pw_im.reshape(L + 1, nb, gpb * n)],
                         axis=-1).transpose(1, 0, 2)
    pw_col = jnp.concatenate([pw_re.reshape(L + 1, nb, gpb * n), pw_im.reshape(L + 1, nb, gpb * n)],
                             axis=0).transpose(1, 2, 0)
    d_row = jnp.tile(d_skip.reshape(nb, 1, LANES), (1, 1, L))
    return bb_bd, c_bd, pw, pw_col, d_row


def _steps_to_lanes(u_ref, stage_ref):
    stage_ref[...] = u_ref[...].astype(F32)
    rows = stage_ref.shape[0] // SSM_CHUNK
    return jnp.concatenate([stage_ref[pl.ds(i, rows, stride=SSM_CHUNK), :]
                            for i in range(SSM_CHUNK)], axis=1)


def _ssm_state_kernel(u_ref, bbr_ref, bbi_ref, pw_ref, h_ref, w_ref, stage_ref, v_ref,
                      *, batch, chunks):
    r = pl.program_id(1)

    @pl.when(r == 0)
    def _():
        half = bbr_ref.shape[-1]
        bbr = bbr_ref[...]
        bbi = bbi_ref[...]
        for i in range(SSM_CHUNK):
            k = SSM_CHUNK - 1 - i
            pr = pw_ref[k:k + 1, 0:half]
            pi = pw_ref[k:k + 1, half:]
            w_ref[i * LANES:(i + 1) * LANES, 0:half] = (pr * bbr - pi * bbi).astype(BF16)
            w_ref[i * LANES:(i + 1) * LANES, half:] = (pr * bbi + pi * bbr).astype(BF16)

    u = _steps_to_lanes(u_ref, stage_ref).astype(BF16)
    tr = u.shape[0]
    v_ref[pl.ds(pl.multiple_of(r * tr, tr), tr), :] = jnp.dot(u, w_ref[...],
                                                             preferred_element_type=F32)

    @pl.when(r == pl.num_programs(1) - 1)
    def _():
        half = v_ref.shape[-1] // 2
        ar = jnp.broadcast_to(pw_ref[SSM_CHUNK:SSM_CHUNK + 1, 0:half], (batch, half))
        ai = jnp.broadcast_to(pw_ref[SSM_CHUNK:SSM_CHUNK + 1, half:], (batch, half))

        def step(c, carry):
            hr, hi = carry
            rows = pl.ds(pl.multiple_of(c * batch, batch), batch)
            h_ref[rows, 0:half] = hr
            h_ref[rows, half:] = hi
            return (ar * hr - ai * hi + v_ref[rows, 0:half],
                    ar * hi + ai * hr + v_ref[rows, half:])

        zero = jnp.zeros((batch, half), F32)
        lax.fori_loop(0, chunks, step, (zero, zero))


def _ssm_out_kernel(u_ref, h_ref, bbr_ref, bbi_ref, cr_ref, ci_ref, pw_ref, pc_ref, d_ref, o_ref,
                    t_ref, z_ref, stage_ref):
    @pl.when(pl.program_id(1) == 0)
    def _():
        L = SSM_CHUNK
        half = cr_ref.shape[0]
        bbr = bbr_ref[...]
        bbi = bbi_ref[...]
        cr = cr_ref[...]
        ci = ci_ref[...]
        ab = []
        for tau in range(L):
            pr = pw_ref[tau:tau + 1, 0:half]
            pi = pw_ref[tau:tau + 1, half:]
            ab.append(jnp.concatenate([pr * bbr - pi * bbi, pr * bbi + pi * bbr], axis=1))
        kern = jnp.dot(jnp.concatenate(ab, axis=0), jnp.concatenate([cr, -ci], axis=0),
                       precision=lax.Precision.HIGHEST, preferred_element_type=F32)
        lag = [kern[tau * LANES:(tau + 1) * LANES, :].astype(BF16) for tau in range(L)]
        zero = jnp.zeros((LANES, LANES), BF16)
        for i in range(L):
            for j in range(L):
                t_ref[i * LANES:(i + 1) * LANES, j * LANES:(j + 1) * LANES] = (
                    lag[j - i] if j >= i else zero)
        for j in range(L):
            pr = pc_ref[:, j + 1:j + 2]
            pi = pc_ref[:, L + j + 2:L + j + 3]
            z_ref[0:half, j * LANES:(j + 1) * LANES] = (cr * pr - ci * pi).astype(BF16)
            z_ref[half:, j * LANES:(j + 1) * LANES] = (-(cr * pi + ci * pr)).astype(BF16)

    u = _steps_to_lanes(u_ref, stage_ref)
    ub = u.astype(BF16)
    mid = t_ref.shape[0] // 2
    toep = jnp.concatenate(
        [jnp.dot(ub[:, 0:mid], t_ref[0:mid, 0:mid], preferred_element_type=F32),
         jnp.dot(ub, t_ref[:, mid:], preferred_element_type=F32)], axis=1)
    y = (toep
         + jnp.dot(h_ref[...].astype(BF16), z_ref[...], preferred_element_type=F32)
         + u * d_ref[...])
    z = jax.nn.gelu(y)
    rows = z.shape[0]
    for i in range(SSM_CHUNK):
        stage_ref[pl.ds(i, rows, stride=SSM_CHUNK), :] = z[:, i * LANES:(i + 1) * LANES]
    o_ref[...] = stage_ref[...].astype(o_ref.dtype)


def _ssm(u, weights, *, layer, batch):
    (bb_r, bb_i), (c_r, c_i), pw, pw_col, d_row = weights
    nb, tokens, _ = u.shape
    m = tokens // SSM_CHUNK
    width = SSM_CHUNK * LANES
    states = pw.shape[-1]
    tr = min(m, 512)
    chunks = m // batch
    grow = lambda g, r: (g, r, 0)
    u_spec = pl.BlockSpec((None, tr * SSM_CHUNK, LANES), grow)
    stage = pltpu.VMEM((tr * SSM_CHUNK, LANES), F32)

    def whole(a):
        return pl.BlockSpec((None, None) + a.shape[2:],
                            lambda g, r: (layer, g) + (0,) * (a.ndim - 2))

    h = pl.pallas_call(
        functools.partial(_ssm_state_kernel, batch=batch, chunks=chunks),
        out_shape=jax.ShapeDtypeStruct((nb, m, states), F32),
        grid=(nb, m // tr),
        in_specs=[u_spec, whole(bb_r), whole(bb_i), whole(pw)],
        out_specs=pl.BlockSpec((None, m, states), lambda g, r: (g, 0, 0)),
        scratch_shapes=[pltpu.VMEM((width, states), BF16), stage, pltpu.VMEM((m, states), F32)],
        compiler_params=_cparams("parallel", "arbitrary"),
        name="ssm_state",
    )(u, bb_r, bb_i, pw)

    return pl.pallas_call(
        _ssm_out_kernel,
        out_shape=jax.ShapeDtypeStruct(u.shape, BF16),
        grid=(nb, m // tr),
        in_specs=[u_spec,
                  pl.BlockSpec((None, tr, states), grow),
                  whole(bb_r), whole(bb_i), whole(c_r), whole(c_i), whole(pw), whole(pw_col),
                  whole(d_row)],
        out_specs=u_spec,
        scratch_shapes=[pltpu.VMEM((width, width), BF16), pltpu.VMEM((states, width), BF16),
                        stage],
        compiler_params=_cparams("parallel", "arbitrary"),
        name="ssm_out",
    )(u, h, bb_r, bb_i, c_r, c_i, pw, pw_col, d_row)


def _merge_kernel(o_ref, z_ref, ga_ref, gb_ref, wo_ref, w1_ref, w2_ref, b1_ref, b2_ref, m_ref):
    tm = o_ref.shape[0]
    z = jnp.concatenate([z_ref[g].reshape(tm, LANES) for g in range(z_ref.shape[0])], axis=1)
    z1 = jnp.dot(z, w1_ref[...], preferred_element_type=F32) + b1_ref[...]
    z2 = jnp.dot(z, w2_ref[...], preferred_element_type=F32) + b2_ref[...]
    s = z1 * jax.nn.sigmoid(z2)
    a = jnp.dot(o_ref[...], wo_ref[...], preferred_element_type=F32)
    m_ref[...] = (ga_ref[...].astype(F32) * a + gb_ref[...].astype(F32) * s).astype(m_ref.dtype)


def _merge(o, z_cb, proj, w_o, w_glu, b_glu, *, layer, seq):
    t, d = o.shape
    nb = z_cb.shape[0]
    zw = nb * LANES
    tm = min(seq, 1024)
    per_b = seq // tm
    tn = 512
    nc = d // tn
    ga0 = SMALL_W // tn
    return pl.pallas_call(
        _merge_kernel,
        out_shape=jax.ShapeDtypeStruct((t, d), BF16),
        grid=(t // tm, nc),
        in_specs=[pl.BlockSpec((tm, d), lambda i, j: (i, 0)),
                  pl.BlockSpec((nb, tm // SSM_CHUNK, SSM_CHUNK, LANES),
                               lambda i, j: (0, i % per_b, i // per_b, 0)),
                  pl.BlockSpec((tm, tn), lambda i, j: (i, ga0 + j)),
                  pl.BlockSpec((tm, tn), lambda i, j: (i, ga0 + nc + j)),
                  pl.BlockSpec((None, d, tn), lambda i, j: (layer, 0, j)),
                  pl.BlockSpec((None, zw, tn), lambda i, j: (layer, 0, j)),
                  pl.BlockSpec((None, zw, tn), lambda i, j: (layer, 0, nc + j)),
                  pl.BlockSpec((1, tn), lambda i, j: (0, j)),
                  pl.BlockSpec((1, tn), lambda i, j: (0, nc + j))],
        out_specs=pl.BlockSpec((tm, tn), lambda i, j: (i, j)),
        compiler_params=_cparams("parallel", "arbitrary"),
        name="merge",
    )(o, z_cb, proj, proj, w_o, w_glu, w_glu, b_glu, b_glu)


def _out_proj_kernel(m_ref, x_ref, w_ref, nw_ref, o_ref):
    mix = jnp.dot(m_ref[...], w_ref[...], preferred_element_type=F32)
    o_ref[...] = x_ref[...] + _rms(mix, nw_ref[...])


def _out_proj(merged, x, w_out, norm_w, *, layer):
    t, d = x.shape
    tm = min(t, 512)
    return pl.pallas_call(
        _out_proj_kernel,
        out_shape=jax.ShapeDtypeStruct((t, d), F32),
        grid=(t // tm,),
        in_specs=[pl.BlockSpec((tm, d), lambda i: (i, 0)),
                  pl.BlockSpec((tm, d), lambda i: (i, 0)),
                  pl.BlockSpec((None, d, d), lambda i: (layer, 0, 0)),
                  pl.BlockSpec((1, d), lambda i: (0, 0))],
        out_specs=pl.BlockSpec((tm, d), lambda i: (i, 0)),
        compiler_params=_cparams("parallel"),
        name="out_proj",
    )(merged, x, w_out, norm_w)


def _ffn_kernel(x_ref, nw_ref, wg_ref, wu_ref, wd_ref, pw_ref, o_ref, h_ref, acc_ref):
    j = pl.program_id(1)

    @pl.when(j == 0)
    def _():
        h_ref[...] = _rms(x_ref[...], nw_ref[...]).astype(BF16)
        acc_ref[...] = jnp.zeros(acc_ref.shape, F32)

    h = h_ref[...]
    gate = jnp.dot(h, wg_ref[...], preferred_element_type=F32)
    up = jnp.dot(h, wu_ref[...], preferred_element_type=F32)
    act = (jax.nn.silu(gate) * up).astype(BF16)
    acc_ref[...] += jnp.dot(act, wd_ref[...], preferred_element_type=F32)

    @pl.when(j == pl.num_programs(1) - 1)
    def _():
        o_ref[...] = x_ref[...] + _rms(acc_ref[...], pw_ref[...])


def _ffn(x, pre_w, w_gate, w_up, w_down, post_w, *, layer):
    t, d = x.shape
    f = w_gate.shape[2]
    tm = min(t, 512)
    tf = 512
    return pl.pallas_call(
        _ffn_kernel,
        out_shape=jax.ShapeDtypeStruct((t, d), F32),
        grid=(t // tm, f // tf),
        in_specs=[pl.BlockSpec((tm, d), lambda i, j: (i, 0)),
                  pl.BlockSpec((1, d), lambda i, j: (0, 0)),
                  pl.BlockSpec((None, d, tf), lambda i, j: (layer, 0, j)),
                  pl.BlockSpec((None, d, tf), lambda i, j: (layer, 0, j)),
                  pl.BlockSpec((None, tf, d), lambda i, j: (layer, j, 0)),
                  pl.BlockSpec((1, d), lambda i, j: (0, 0))],
        out_specs=pl.BlockSpec((tm, d), lambda i, j: (i, 0)),
        scratch_shapes=[pltpu.VMEM((tm, d), BF16), pltpu.VMEM((tm, d), F32)],
        compiler_params=_cparams("parallel", "arbitrary"),
        name="ffn",
    )(x, pre_w, w_gate, w_up, w_down, post_w)


def _cast_kernel(x_ref, o_ref):
    o_ref[...] = x_ref[...].astype(o_ref.dtype)


def _to_bf16(w):
    return _regroup_call(_cast_kernel, w, [w.shape[-1]], "to_bf16")[0]


def _regroup_call(body, w, out_cols, name):
    layers, r, c = w.shape
    tr = r
    while tr * c * 4 > CAST_BLOCK_BYTES and tr % 32 == 0:
        tr //= 2
    outs = pl.pallas_call(
        body,
        out_shape=[jax.ShapeDtypeStruct((layers, r, n), BF16) for n in out_cols],
        grid=(layers, r // tr),
        in_specs=[pl.BlockSpec((None, tr, c), lambda l, i: (l, i, 0))],
        out_specs=[pl.BlockSpec((None, tr, n), lambda l, i: (l, i, 0)) for n in out_cols],
        compiler_params=_cparams("parallel", "parallel"),
        name=name,
    )(w)
    return outs


def _w_in_body(x_ref, o_ref):
    off_ssm = OFF_KPE + QK_ROPE_DIM
    half = QK_ROPE_DIM // 2
    gates = o_ref.shape[-1] - 2 * SMALL_W
    off_gate = x_ref.shape[-1] - gates
    ssm_w = off_gate - off_ssm
    x = x_ref[...].astype(BF16)
    o_ref[:, 0:ssm_w] = x[:, off_ssm:off_gate]
    o_ref[:, ssm_w:ssm_w + off_ssm] = x[:, 0:off_ssm]
    base = ssm_w + off_ssm
    o_ref[:, base:base + half] = x[:, OFF_KPE + half:off_ssm]
    o_ref[:, base + half:base + 2 * half] = x[:, OFF_KPE:OFF_KPE + half]
    o_ref[:, base + 2 * half:2 * SMALL_W] = jnp.zeros(
        (x.shape[0], 2 * SMALL_W - base - 2 * half), BF16)
    o_ref[:, 2 * SMALL_W:] = x[:, off_gate:]


def _w_uq_body(x_ref, o_ref):
    x = x_ref[...].astype(BF16)
    qk = QK_NOPE_DIM + QK_ROPE_DIM
    half = QK_ROPE_DIM // 2
    for h in range(MLA_HEADS):
        src = h * qk
        dst = h * HEAD_PAD
        o_ref[:, dst:dst + qk] = x[:, src:src + qk]
        o_ref[:, dst + qk:dst + qk + half] = x[:, src + QK_NOPE_DIM + half:src + qk]
        o_ref[:, dst + qk + half:dst + HEAD_PAD] = x[:, src + QK_NOPE_DIM:src + QK_NOPE_DIM + half]


def _w_ukv_body(x_ref, k_ref, v_ref):
    x = x_ref[...].astype(BF16)
    per = QK_NOPE_DIM + V_HEAD_DIM
    for h in range(MLA_HEADS):
        k_ref[:, h * QK_NOPE_DIM:(h + 1) * QK_NOPE_DIM] = x[:, h * per:h * per + QK_NOPE_DIM]
        v_ref[:, h * V_HEAD_DIM:(h + 1) * V_HEAD_DIM] = x[:, h * per + QK_NOPE_DIM:(h + 1) * per]


def _regroup_w_in(w_in):
    off_ssm = OFF_KPE + QK_ROPE_DIM
    return _regroup_call(_w_in_body, w_in, [w_in.shape[-1] - off_ssm + SMALL_W], "prep_w_in")[0]


def _regroup_w_uq(w_uq):
    return _regroup_call(_w_uq_body, w_uq, [MLA_HEADS * HEAD_PAD], "prep_w_uq")[0]


def _regroup_w_ukv(w_ukv):
    return _regroup_call(_w_ukv_body, w_ukv,
                         [MLA_HEADS * QK_NOPE_DIM, MLA_HEADS * V_HEAD_DIM], "prep_w_ukv")


def kernel(x, positions, pre_mix_norm, w_in, b_gate, q_norm, kv_norm, w_uq, w_ukv, w_o_mla,
           ssm_a_re, ssm_a_im, ssm_log_dt, ssm_b_re, ssm_b_im, ssm_c_re, ssm_c_im, ssm_d,
           w_glu, b_glu, w_out, post_mix_norm, pre_ffn_norm, w_ffn_gate, w_ffn_up,
           w_ffn_down, post_ffn_norm):
    batch, seq, d = x.shape
    depth = w_in.shape[0]
    t = batch * seq
    ssm_w = ssm_a_re.shape[1] * SSM_GROUP
    chunks = seq // SSM_CHUNK
    assert seq % max(CHUNK, SSM_CHUNK) == 0 and batch % 8 == 0
    assert ssm_w == SMALL_W and d % SMALL_W == 0

    w_in_p = _regroup_w_in(w_in)
    bias = jnp.concatenate([jnp.zeros((depth, w_in_p.shape[-1] - b_gate.shape[-1]), F32), b_gate],
                           axis=-1)
    wq = _regroup_w_uq(w_uq)
    wk, wv = _regroup_w_ukv(w_ukv)
    w_o_b, w_glu_b, w_out_b = _to_bf16(w_o_mla), _to_bf16(w_glu), _to_bf16(w_out)
    w_fg, w_fu, w_fd = _to_bf16(w_ffn_gate), _to_bf16(w_ffn_up), _to_bf16(w_ffn_down)
    ssm_weights = jax.vmap(_ssm_weights)(ssm_a_re, ssm_a_im, ssm_log_dt, ssm_b_re, ssm_b_im,
                                         ssm_c_re, ssm_c_im, ssm_d)

    tab = _rope_table(positions)
    xf = x.reshape(t, d)
    row = lambda v: v.reshape(1, -1)
    for l in range(depth):
        proj, u_cb = _in_proj(xf, row(pre_mix_norm[l]), w_in_p, row(bias[l]),
                              layer=l, batch=batch, seq=seq)
        q, k, v = _qkv_up(proj, tab, row(q_norm[l]), row(kv_norm[l]), wq, wk, wv,
                          layer=l, batch=batch, seq=seq)
        o = _attention(q, k, v).reshape(t, MLA_HEADS * V_HEAD_DIM)

        nb = u_cb.shape[0]
        z_cb = _ssm(u_cb.reshape(nb, chunks * batch * SSM_CHUNK, LANES), ssm_weights,
                    layer=l, batch=batch)
        z_cb = z_cb.reshape(u_cb.shape)

        merged = _merge(o, z_cb, proj, w_o_b, w_glu_b, row(b_glu[l]), layer=l, seq=seq)
        xf = _out_proj(merged, xf, w_out_b, row(post_mix_norm[l]), layer=l)
        xf = _ffn(xf, row(pre_ffn_norm[l]), w_fg, w_fu, w_fd, row(post_ffn_norm[l]), layer=l)
    return xf.reshape(batch, seq, d)
```

```python
import functools
import math

import jax
import jax.numpy as jnp
from jax import lax
from jax.experimental import pallas as pl
from jax.experimental.pallas import tpu as pltpu

F32 = jnp.float32
BF16 = jnp.bfloat16

CHUNK = 64
MLA_HEADS = 16
QK_NOPE_DIM = 128
QK_ROPE_DIM = 64
V_HEAD_DIM = 128
Q_LORA_RANK = 512
KV_LORA_RANK = 256
ROPE_THETA = 10000.0
SSM_GROUP = 16
SSM_STATE = 64
EPS = 1e-6

LANES = 128
HEAD_PAD = 256
SSM_CHUNK = 16
GROUPS_PER_BLOCK = LANES // SSM_GROUP
VMEM_LIMIT = 56 * 1024 * 1024
CAST_BLOCK_BYTES = 8 * 1024 * 1024
NEG_BIG = -1e30

SMALL_W = 1024
OFF_KPE = Q_LORA_RANK + KV_LORA_RANK


def _cparams(*sem):
    return pltpu.CompilerParams(dimension_semantics=sem, vmem_limit_bytes=VMEM_LIMIT)


def _rms(x, w):
    return x * lax.rsqrt(jnp.mean(x * x, axis=-1, keepdims=True) + EPS) * w


def _rope_table_kernel(pos_ref, freq_ref, tab_ref):
    ang = pos_ref[...].astype(F32) * freq_ref[...]
    lane = lax.broadcasted_iota(jnp.int32, ang.shape, 1)
    c = jnp.cos(ang)
    s = jnp.sin(ang)
    tab_ref[...] = jnp.where(lane < 64, c, jnp.where(lane < 96, -s, s))


def _rope_table(positions):
    t = positions.size
    tm = min(t, 2048)
    half = QK_ROPE_DIM // 2
    inv_freq = ROPE_THETA ** (-jnp.arange(0, QK_ROPE_DIM, 2, dtype=F32) / QK_ROPE_DIM)
    freq = jnp.tile(inv_freq, LANES // half).reshape(1, LANES)
    return pl.pallas_call(
        _rope_table_kernel,
        out_shape=jax.ShapeDtypeStruct((t, LANES), F32),
        grid=(t // tm,),
        in_specs=[pl.BlockSpec((tm, 1), lambda i: (i, 0)),
                  pl.BlockSpec((1, LANES), lambda i: (0, 0))],
        out_specs=pl.BlockSpec((tm, LANES), lambda i: (i, 0)),
        compiler_params=_cparams("parallel"),
        name="rope_table",
    )(positions.reshape(t, 1), freq)


def _in_proj_kernel(x_ref, nw_ref, w_ref, b_ref, p_ref, u_ref, h_ref):
    j = pl.program_id(1)

    @pl.when(j == 0)
    def _():
        h_ref[...] = _rms(x_ref[...], nw_ref[...]).astype(BF16)

    acc = jnp.dot(h_ref[...], w_ref[...], preferred_element_type=F32)

    gated = jax.nn.sigmoid(acc + b_ref[...])
    p_ref[...] = jnp.where(j >= 2, gated, acc).astype(BF16)

    @pl.when(j == 0)
    def _():
        u = acc.astype(BF16)
        for g in range(u_ref.shape[0]):
            for c in range(u_ref.shape[1]):
                u_ref[g, c] = u[c * SSM_CHUNK:(c + 1) * SSM_CHUNK, g * LANES:(g + 1) * LANES]


def _in_proj(x, norm_w, w, bias, *, layer, batch, seq):
    t, d = x.shape
    n = w.shape[2]
    tn = SMALL_W
    tm = min(seq, 1024)
    per_b = seq // tm
    cpt = tm // SSM_CHUNK
    nb = tn // LANES
    return pl.pallas_call(
        _in_proj_kernel,
        out_shape=(jax.ShapeDtypeStruct((t, n - tn), BF16),
                   jax.ShapeDtypeStruct((nb, seq // SSM_CHUNK, batch * SSM_CHUNK, LANES), BF16)),
        grid=(t // tm, n // tn),
        in_specs=[pl.BlockSpec((tm, d), lambda i, j: (i, 0)),
                  pl.BlockSpec((1, d), lambda i, j: (0, 0)),
                  pl.BlockSpec((None, d, tn), lambda i, j: (layer, 0, j)),
                  pl.BlockSpec((1, tn), lambda i, j: (0, j))],
        out_specs=(pl.BlockSpec((tm, tn), lambda i, j: (i, jnp.maximum(j - 1, 0))),
                   pl.BlockSpec((nb, cpt, SSM_CHUNK, LANES),
                                lambda i, j: (0, i % per_b, i // per_b, 0))),
        scratch_shapes=[pltpu.VMEM((tm, d), BF16)],
        compiler_params=_cparams("parallel", "arbitrary"),
        name="in_proj",
    )(x, norm_w, w, bias)


def _qkv_up_kernel(p_ref, tab_ref, qn_ref, kvn_ref, wq_ref, wk_ref, wv_ref,
                   q_ref, k_ref, v_ref, *, scale):
    tab = tab_ref[...]
    tabs = tab * scale
    cq = _rms(p_ref[:, 0:Q_LORA_RANK].astype(F32), qn_ref[...]).astype(BF16)
    q_all = jnp.dot(cq, wq_ref[...], preferred_element_type=F32)
    for h in range(MLA_HEADS):
        base = h * HEAD_PAD
        q_ref[h, :, 0:QK_NOPE_DIM] = (q_all[:, base:base + QK_NOPE_DIM] * scale).astype(BF16)
        q_ref[h, :, QK_NOPE_DIM:HEAD_PAD] = (
            q_all[:, base + QK_NOPE_DIM:base + HEAD_PAD] * tabs).astype(BF16)

    ckv = _rms(p_ref[:, Q_LORA_RANK:OFF_KPE].astype(F32), kvn_ref[...]).astype(BF16)
    k_all = jnp.dot(ckv, wk_ref[...], preferred_element_type=F32)
    v_all = jnp.dot(ckv, wv_ref[...], preferred_element_type=F32)
    kt = p_ref[:, OFF_KPE:OFF_KPE + LANES].astype(F32) * tab
    kr = (kt + pltpu.roll(kt, LANES // 2, 1)).astype(BF16)
    ones = jnp.ones((kr.shape[0], LANES), BF16)
    for h in range(MLA_HEADS):
        k_ref[h, :, 0:QK_NOPE_DIM] = k_all[:, h * QK_NOPE_DIM:(h + 1) * QK_NOPE_DIM].astype(BF16)
        k_ref[h, :, QK_NOPE_DIM:HEAD_PAD] = kr
        v_ref[h, :, 0:V_HEAD_DIM] = v_all[:, h * V_HEAD_DIM:(h + 1) * V_HEAD_DIM].astype(BF16)
        v_ref[h, :, V_HEAD_DIM:HEAD_PAD] = ones


def _qkv_up(proj, tab, q_norm, kv_norm, wq, wk, wv, *, layer, batch, seq):
    tm = min(seq, 512)
    nb = seq // tm
    hshape = jax.ShapeDtypeStruct((batch, MLA_HEADS, seq, HEAD_PAD), BF16)
    hspec = pl.BlockSpec((None, MLA_HEADS, tm, HEAD_PAD), lambda b, i: (b, 0, i, 0))
    scale = (QK_NOPE_DIM + QK_ROPE_DIM) ** -0.5 * math.log2(math.e)
    const = lambda b, i: (0, 0)
    return pl.pallas_call(
        functools.partial(_qkv_up_kernel, scale=scale),
        out_shape=(hshape, hshape, hshape),
        grid=(batch, nb),
        in_specs=[pl.BlockSpec((tm, SMALL_W), lambda b, i: (b * nb + i, 0)),
                  pl.BlockSpec((tm, LANES), lambda b, i: (b * nb + i, 0)),
                  pl.BlockSpec((1, Q_LORA_RANK), const),
                  pl.BlockSpec((1, KV_LORA_RANK), const),
                  pl.BlockSpec((None,) + wq.shape[1:], lambda b, i: (layer, 0, 0)),
                  pl.BlockSpec((None,) + wk.shape[1:], lambda b, i: (layer, 0, 0)),
                  pl.BlockSpec((None,) + wv.shape[1:], lambda b, i: (layer, 0, 0))],
        out_specs=(hspec, hspec, hspec),
        compiler_params=_cparams("parallel", "parallel"),
        name="qkv_up",
    )(proj, tab, q_norm, kv_norm, wq, wk, wv)


def _attn_kernel(q_ref, k_ref, v_ref, o_ref, *, tq):
    seq = q_ref.shape[0]
    per = tq // LANES
    row_chunk = lax.broadcasted_iota(jnp.int32, (tq, LANES), 0) // CHUNK
    lane = lax.broadcasted_iota(jnp.int32, (tq, LANES), 1)
    masks = [((c * LANES + lane) // CHUNK) <= row_chunk for c in range(per)]
    nq = seq // tq
    order = [t for pair in zip(range(nq - 1, -1, -1), range(nq)) for t in pair][:nq]
    for i in order:
        kv = (i + 1) * tq
        q = q_ref[i * tq:(i + 1) * tq, :]
        s = lax.dot_general(q, k_ref[0:kv, :], (((1,), (1,)), ((), ())),
                            preferred_element_type=F32)
        pieces = []
        for c in range(kv // LANES):
            piece = s[:, c * LANES:(c + 1) * LANES]
            if c >= i * per:
                piece = jnp.where(masks[c - i * per], piece, NEG_BIG)
            pieces.append(piece)
        mx = pieces[0]
        for piece in pieces[1:]:
            mx = jnp.maximum(mx, piece)
        m = jnp.broadcast_to(jnp.max(mx, axis=1, keepdims=True), (tq, LANES))
        p = jnp.concatenate([jnp.exp2(piece - m) for piece in pieces], axis=1).astype(BF16)
        acc = jnp.dot(p, v_ref[0:kv, :], preferred_element_type=F32)
        o_ref[i * tq:(i + 1) * tq, :] = (
            acc[:, 0:V_HEAD_DIM] / acc[:, V_HEAD_DIM:HEAD_PAD]).astype(o_ref.dtype)


def _attention(q, k, v):
    batch, heads, seq, _ = q.shape
    tq = min(seq, 512)
    spec = pl.BlockSpec((None, None, seq, HEAD_PAD), lambda b, h: (b, h, 0, 0))
    return pl.pallas_call(
        functools.partial(_attn_kernel, tq=tq),
        out_shape=jax.ShapeDtypeStruct((batch, seq, heads * V_HEAD_DIM), BF16),
        grid=(batch, heads),
        in_specs=[spec, spec, spec],
        out_specs=pl.BlockSpec((None, seq, V_HEAD_DIM), lambda b, h: (b, 0, h)),
        compiler_params=_cparams("parallel", "parallel"),
        name="attn",
    )(q, k, v)


def _ssm_weights(a_re, a_im, log_dt, b_re, b_im, c_re, c_im, d_skip):
    L = SSM_CHUNK
    g, n = a_re.shape
    gpb = GROUPS_PER_BLOCK
    nb = g // gpb
    dt = jnp.exp(log_dt)[:, None]
    mag = jnp.exp(a_re * dt)
    abar_re = mag * jnp.cos(a_im * dt)
    abar_im = mag * jnp.sin(a_im * dt)
    den = a_re * a_re + a_im * a_im
    nr = abar_re - 1.0
    f_re = (nr * a_re + abar_im * a_im) / den
    f_im = (abar_im * a_re - nr * a_im) / den
    bb_re = f_re[..., None] * b_re - f_im[..., None] * b_im
    bb_im = f_re[..., None] * b_im + f_im[..., None] * b_re
    kk = jnp.arange(L + 1, dtype=F32)[:, None, None]
    pmag = jnp.exp(kk * (a_re * dt))
    pw_re = pmag * jnp.cos(kk * (a_im * dt))
    pw_im = pmag * jnp.sin(kk * (a_im * dt))
    eye = jnp.eye(gpb, dtype=F32)

    def block_diag(a):
        lead = a.shape[:-3]
        r, c = a.shape[-2:]
        a = a.reshape(lead + (nb, gpb, r, 1, c)) * eye[:, None, :, None]
        return a.reshape(lead + (nb, gpb * r, gpb * c))

    bb_bd =(block_diag(bb_re.transpose(0, 2, 1)), block_diag(bb_im.transpose(0, 2, 1)))
    c_bd = (block_diag(c_re.transpose(0, 2, 1)), block_diag(c_im.transpose(0, 2, 1)))
    pw = jnp.concatenate([pw_re.reshape(L + 1, nb, gpb * n), pw_im.reshape(L + 1, nb, gpb * n)],
                         axis=-1).transpose(1, 0, 2)
    pw_col = jnp.concatenate([pw_re.reshape(L + 1, nb, gpb * n), pw_im.reshape(L + 1, nb, gpb * n)],
                             axis=0).transpose(1, 2, 0)
    d_row = jnp.tile(d_skip.reshape(nb, 1, LANES), (1, 1, L))
    return bb_bd, c_bd, pw, pw_col, d_row


def _steps_to_lanes(u_ref, stage_ref):
    stage_ref[...] = u_ref[...].astype(F32)
    rows = stage_ref.shape[0] // SSM_CHUNK
    return jnp.concatenate([stage_ref[pl.ds(i, rows, stride=SSM_CHUNK), :]
                            for i in range(SSM_CHUNK)], axis=1)


def _ssm_state_kernel(u_ref, bbr_ref, bbi_ref, pw_ref, h_ref, w_ref, stage_ref, v_ref,
                      *, batch, chunks):
    r = pl.program_id(1)

    @pl.when(r == 0)
    def _():
        half = bbr_ref.shape[-1]
        bbr = bbr_ref[...]
        bbi = bbi_ref[...]
        for i in range(SSM_CHUNK):
            k = SSM_CHUNK - 1 - i
            pr = pw_ref[k:k + 1, 0:half]
            pi = pw_ref[k:k + 1, half:]
            w_ref[i * LANES:(i + 1) * LANES, 0:half] = (pr * bbr - pi * bbi).astype(BF16)
            w_ref[i * LANES:(i + 1) * LANES, half:] = (pr * bbi + pi * bbr).astype(BF16)

    u = _steps_to_lanes(u_ref, stage_ref).astype(BF16)
    tr = u.shape[0]
    v_ref[pl.ds(pl.multiple_of(r * tr, tr), tr), :] = jnp.dot(u, w_ref[...],
                                                             preferred_element_type=F32)

    @pl.when(r == pl.num_programs(1) - 1)
    def _():
        half = v_ref.shape[-1] // 2
        ar = jnp.broadcast_to(pw_ref[SSM_CHUNK:SSM_CHUNK + 1, 0:half], (batch, half))
        ai = jnp.broadcast_to(pw_ref[SSM_CHUNK:SSM_CHUNK + 1, half:], (batch, half))

        def step(c, carry):
            hr, hi = carry
            rows = pl.ds(pl.multiple_of(c * batch, batch), batch)
            h_ref[rows, 0:half] = hr
            h_ref[rows, half:] = hi
            return (ar * hr - ai * hi + v_ref[rows, 0:half],
                    ar * hi + ai * hr + v_ref[rows, half:])

        zero = jnp.zeros((batch, half), F32)
        lax.fori_loop(0, chunks, step, (zero, zero))


def _ssm_out_kernel(u_ref, h_ref, bbr_ref, bbi_ref, cr_ref, ci_ref, pw_ref, pc_ref, d_ref, o_ref,
                    t_ref, z_ref, stage_ref):
    @pl.when(pl.program_id(1) == 0)
    def _():
        L = SSM_CHUNK
        half = cr_ref.shape[0]
        bbr = bbr_ref[...]
        bbi = bbi_ref[...]
        cr = cr_ref[...]
        ci = ci_ref[...]
        ab = []
        for tau in range(L):
            pr = pw_ref[tau:tau + 1, 0:half]
            pi = pw_ref[tau:tau + 1, half:]
            ab.append(jnp.concatenate([pr * bbr - pi * bbi, pr * bbi + pi * bbr], axis=1))
        kern = jnp.dot(jnp.concatenate(ab, axis=0).astype(BF16),
                       jnp.concatenate([cr, -ci], axis=0).astype(BF16),
                       preferred_element_type=F32)
        lag = [kern[tau * LANES:(tau + 1) * LANES, :].astype(BF16) for tau in range(L)]
        zero = jnp.zeros((LANES, LANES), BF16)
        for i in range(L):
            for j in range(L):
                t_ref[i * LANES:(i + 1) * LANES, j * LANES:(j + 1) * LANES] = (
                    lag[j - i] if j >= i else zero)
        for j in range(L):
            pr = pc_ref[:, j + 1:j + 2]
            pi = pc_ref[:, L + j + 2:L + j + 3]
            z_ref[0:half, j * LANES:(j + 1) * LANES] = (cr * pr - ci * pi).astype(BF16)
            z_ref[half:, j * LANES:(j + 1) * LANES] = (-(cr * pi + ci * pr)).astype(BF16)

    u = _steps_to_lanes(u_ref, stage_ref)
    ub = u.astype(BF16)
    mid = t_ref.shape[0] // 2
    toep = jnp.concatenate(
        [jnp.dot(ub[:, 0:mid], t_ref[0:mid, 0:mid], preferred_element_type=F32),
         jnp.dot(ub, t_ref[:, mid:], preferred_element_type=F32)], axis=1)
    y = (toep
         + jnp.dot(h_ref[...].astype(BF16), z_ref[...], preferred_element_type=F32)
         + u * d_ref[...])
    z = jax.nn.gelu(y)
    rows = z.shape[0]
    for i in range(SSM_CHUNK):
        stage_ref[pl.ds(i, rows, stride=SSM_CHUNK), :] = z[:, i * LANES:(i + 1) * LANES]
    o_ref[...] = stage_ref[...].astype(o_ref.dtype)


def _ssm(u, weights, *, layer, batch):
    (bb_r, bb_i), (c_r, c_i), pw, pw_col, d_row = weights
    nb, tokens, _ = u.shape
    m = tokens // SSM_CHUNK
    width = SSM_CHUNK * LANES
    states = pw.shape[-1]
    tr = min(m, 512)
    chunks = m // batch
    grow = lambda g, r: (g, r, 0)
    u_spec = pl.BlockSpec((None, tr * SSM_CHUNK, LANES), grow)
    stage = pltpu.VMEM((tr * SSM_CHUNK, LANES), F32)

    def whole(a):
        return pl.BlockSpec((None, None) + a.shape[2:],
                            lambda g, r: (layer, g) + (0,) * (a.ndim - 2))

    h = pl.pallas_call(
        functools.partial(_ssm_state_kernel, batch=batch, chunks=chunks),
        out_shape=jax.ShapeDtypeStruct((nb, m, states), F32),
        grid=(nb, m // tr),
        in_specs=[u_spec, whole(bb_r), whole(bb_i), whole(pw)],
        out_specs=pl.BlockSpec((None, m, states), lambda g, r: (g, 0, 0)),
        scratch_shapes=[pltpu.VMEM((width, states), BF16), stage, pltpu.VMEM((m, states), F32)],
        compiler_params=_cparams("parallel", "arbitrary"),
        name="ssm_state",
    )(u, bb_r, bb_i, pw)

    return pl.pallas_call(
        _ssm_out_kernel,
        out_shape=jax.ShapeDtypeStruct(u.shape, BF16),
        grid=(nb, m // tr),
        in_specs=[u_spec,
                  pl.BlockSpec((None, tr, states), grow),
                  whole(bb_r), whole(bb_i), whole(c_r), whole(c_i), whole(pw), whole(pw_col),
                  whole(d_row)],
        out_specs=u_spec,
        scratch_shapes=[pltpu.VMEM((width, width), BF16), pltpu.VMEM((states, width), BF16),
                        stage],
        compiler_params=_cparams("parallel", "arbitrary"),
        name="ssm_out",
    )(u, h, bb_r, bb_i, c_r, c_i, pw, pw_col, d_row)


def _merge_kernel(o_ref, z_ref, ga_ref, gb_ref, wo_ref, w1_ref, w2_ref, b1_ref, b2_ref, m_ref):
    tm = o_ref.shape[0]
    z = jnp.concatenate([z_ref[g].reshape(tm, LANES) for g in range(z_ref.shape[0])], axis=1)
    z1 = jnp.dot(z, w1_ref[...], preferred_element_type=F32) + b1_ref[...]
    z2 = jnp.dot(z, w2_ref[...], preferred_element_type=F32) + b2_ref[...]
    s = z1 * jax.nn.sigmoid(z2)
    a = jnp.dot(o_ref[...], wo_ref[...], preferred_element_type=F32)
    m_ref[...] = (ga_ref[...].astype(F32) * a + gb_ref[...].astype(F32) * s).astype(m_ref.dtype)


def _merge(o, z_cb, proj, w_o, w_glu, b_glu, *, layer, seq):
    t, d = o.shape
    nb = z_cb.shape[0]
    zw = nb * LANES
    tm = min(seq, 1024)
    per_b = seq // tm
    tn = 512
    nc = d // tn
    ga0 = SMALL_W // tn
    return pl.pallas_call(
        _merge_kernel,
        out_shape=jax.ShapeDtypeStruct((t, d), BF16),
        grid=(t // tm, nc),
        in_specs=[pl.BlockSpec((tm, d), lambda i, j: (i, 0)),
                  pl.BlockSpec((nb, tm // SSM_CHUNK, SSM_CHUNK, LANES),
                               lambda i, j: (0, i % per_b, i // per_b, 0)),
                  pl.BlockSpec((tm, tn), lambda i, j: (i, ga0 + j)),
                  pl.BlockSpec((tm, tn), lambda i, j: (i, ga0 + nc + j)),
                  pl.BlockSpec((None, d, tn), lambda i, j: (layer, 0, j)),
                  pl.BlockSpec((None, zw, tn), lambda i, j: (layer, 0, j)),
                  pl.BlockSpec((None, zw, tn), lambda i, j: (layer, 0, nc + j)),
                  pl.BlockSpec((1, tn), lambda i, j: (0, j)),
                  pl.BlockSpec((1, tn), lambda i, j: (0, nc + j))],
        out_specs=pl.BlockSpec((tm, tn), lambda i, j: (i, j)),
        compiler_params=_cparams("parallel", "arbitrary"),
        name="merge",
    )(o, z_cb, proj, proj, w_o, w_glu, w_glu, b_glu, b_glu)


def _out_proj_kernel(m_ref, x_ref, w_ref, nw_ref, o_ref):
    mix = jnp.dot(m_ref[...], w_ref[...], preferred_element_type=F32)
    o_ref[...] = x_ref[...] + _rms(mix, nw_ref[...])


def _out_proj(merged, x, w_out, norm_w, *, layer):
    t, d = x.shape
    tm = min(t, 512)
    return pl.pallas_call(
        _out_proj_kernel,
        out_shape=jax.ShapeDtypeStruct((t, d), F32),
        grid=(t // tm,),
        in_specs=[pl.BlockSpec((tm, d), lambda i: (i, 0)),
                  pl.BlockSpec((tm, d), lambda i: (i, 0)),
                  pl.BlockSpec((None, d, d), lambda i: (layer, 0, 0)),
                  pl.BlockSpec((1, d), lambda i: (0, 0))],
        out_specs=pl.BlockSpec((tm, d), lambda i: (i, 0)),
        compiler_params=_cparams("parallel"),
        name="out_proj",
    )(merged, x, w_out, norm_w)


def _ffn_kernel(x_ref, nw_ref, wgu_ref, wd_ref, pw_ref, o_ref, h_ref, acc_ref):
    j = pl.program_id(1)

    @pl.when(j == 0)
    def _():
        h_ref[...] = _rms(x_ref[...], nw_ref[...]).astype(BF16)
        acc_ref[...] = jnp.zeros(acc_ref.shape, F32)

    tf = wd_ref.shape[0]
    gate_up = jnp.dot(h_ref[...], wgu_ref[...], preferred_element_type=F32)
    act = (jax.nn.silu(gate_up[:, 0:tf]) * gate_up[:, tf:]).astype(BF16)
    acc_ref[...] += jnp.dot(act, wd_ref[...], preferred_element_type=F32)

    @pl.when(j == pl.num_programs(1) - 1)
    def _():
        o_ref[...] = x_ref[...] + _rms(acc_ref[...], pw_ref[...])


FFN_TILE = 512


def _ffn(x, pre_w, w_gate_up, w_down, post_w, *, layer):
    t, d = x.shape
    f = w_down.shape[1]
    tm = min(t, 512)
    tf = FFN_TILE
    return pl.pallas_call(
        _ffn_kernel,
        out_shape=jax.ShapeDtypeStruct((t, d), F32),
        grid=(t // tm, f // tf),
        in_specs=[pl.BlockSpec((tm, d), lambda i, j: (i, 0)),
                  pl.BlockSpec((1, d), lambda i, j: (0, 0)),
                  pl.BlockSpec((None, d, 2 * tf), lambda i, j: (layer, 0, j)),
                  pl.BlockSpec((None, tf, d), lambda i, j: (layer, j, 0)),
                  pl.BlockSpec((1, d), lambda i, j: (0, 0))],
        out_specs=pl.BlockSpec((tm, d), lambda i, j: (i, 0)),
        scratch_shapes=[pltpu.VMEM((tm, d), BF16), pltpu.VMEM((tm, d), F32)],
        compiler_params=_cparams("parallel", "arbitrary"),
        name="ffn",
    )(x, pre_w, w_gate_up, w_down, post_w)


def _pair_tiles_kernel(g_ref, u_ref, o_ref):
    tf = g_ref.shape[-1]
    o_ref[:, 0:tf] = g_ref[...].astype(BF16)
    o_ref[:, tf:] = u_ref[...].astype(BF16)


def _gate_up_bf16(w_gate, w_up):
    layers, d, f = w_gate.shape
    tf = FFN_TILE
    spec = pl.BlockSpec((None, d, tf), lambda l, j: (l, 0, j))
    return pl.pallas_call(
        _pair_tiles_kernel,
        out_shape=jax.ShapeDtypeStruct((layers, d, 2 * f), BF16),
        grid=(layers, f // tf),
        in_specs=[spec, spec],
        out_specs=pl.BlockSpec((None, d, 2 * tf), lambda l, j: (l, 0, j)),
        compiler_params=_cparams("parallel", "parallel"),
        name="gate_up_bf16",
    )(w_gate, w_up)


def _cast_kernel(x_ref, o_ref):
    o_ref[...] = x_ref[...].astype(o_ref.dtype)


def _to_bf16(w):
    return _regroup_call(_cast_kernel, w, [w.shape[-1]], "to_bf16")[0]


def _regroup_call(body, w, out_cols, name):
    layers, r, c = w.shape
    tr = r
    while tr * c * 4 > CAST_BLOCK_BYTES and tr % 32 == 0:
        tr //= 2
    outs = pl.pallas_call(
        body,
        out_shape=[jax.ShapeDtypeStruct((layers, r, n), BF16) for n in out_cols],
        grid=(layers, r // tr),
        in_specs=[pl.BlockSpec((None, tr, c), lambda l, i: (l, i, 0))],
        out_specs=[pl.BlockSpec((None, tr, n), lambda l, i: (l, i, 0)) for n in out_cols],
        compiler_params=_cparams("parallel", "parallel"),
        name=name,
    )(w)
    return outs


def _w_in_body(x_ref, o_ref):
    off_ssm = OFF_KPE + QK_ROPE_DIM
    half = QK_ROPE_DIM // 2
    gates = o_ref.shape[-1] - 2 * SMALL_W
    off_gate = x_ref.shape[-1] - gates
    ssm_w = off_gate - off_ssm
    x = x_ref[...].astype(BF16)
    o_ref[:, 0:ssm_w] = x[:, off_ssm:off_gate]
    o_ref[:, ssm_w:ssm_w + off_ssm] = x[:, 0:off_ssm]
    base = ssm_w + off_ssm
    o_ref[:, base:base + half] = x[:, OFF_KPE + half:off_ssm]
    o_ref[:, base + half:base + 2 * half] = x[:, OFF_KPE:OFF_KPE + half]
    o_ref[:, base + 2 * half:2 * SMALL_W] = jnp.zeros(
        (x.shape[0], 2 * SMALL_W - base - 2 * half), BF16)
    o_ref[:, 2 * SMALL_W:] = x[:, off_gate:]


def _w_uq_body(x_ref, o_ref):
    x = x_ref[...].astype(BF16)
    qk = QK_NOPE_DIM + QK_ROPE_DIM
    half = QK_ROPE_DIM // 2
    for h in range(MLA_HEADS):
        src = h * qk
        dst = h * HEAD_PAD
        o_ref[:, dst:dst + qk] = x[:, src:src + qk]
        o_ref[:, dst + qk:dst + qk + half] = x[:, src + QK_NOPE_DIM + half:src + qk]
        o_ref[:, dst + qk + half:dst + HEAD_PAD] = x[:, src + QK_NOPE_DIM:src + QK_NOPE_DIM + half]


def _w_ukv_body(x_ref, k_ref, v_ref):
    x = x_ref[...].astype(BF16)
    per = QK_NOPE_DIM + V_HEAD_DIM
    for h in range(MLA_HEADS):
        k_ref[:, h * QK_NOPE_DIM:(h + 1) * QK_NOPE_DIM] = x[:, h * per:h * per + QK_NOPE_DIM]
        v_ref[:, h * V_HEAD_DIM:(h + 1) * V_HEAD_DIM] = x[:, h * per + QK_NOPE_DIM:(h + 1) * per]


def _regroup_w_in(w_in):
    off_ssm = OFF_KPE + QK_ROPE_DIM
    return _regroup_call(_w_in_body, w_in, [w_in.shape[-1] - off_ssm + SMALL_W], "prep_w_in")[0]


def _regroup_w_uq(w_uq):
    return _regroup_call(_w_uq_body, w_uq, [MLA_HEADS * HEAD_PAD], "prep_w_uq")[0]


def _regroup_w_ukv(w_ukv):
    return _regroup_call(_w_ukv_body, w_ukv,
                         [MLA_HEADS * QK_NOPE_DIM, MLA_HEADS * V_HEAD_DIM], "prep_w_ukv")


def kernel(x, positions, pre_mix_norm, w_in, b_gate, q_norm, kv_norm, w_uq, w_ukv, w_o_mla,
           ssm_a_re, ssm_a_im, ssm_log_dt, ssm_b_re, ssm_b_im, ssm_c_re, ssm_c_im, ssm_d,
           w_glu, b_glu, w_out, post_mix_norm, pre_ffn_norm, w_ffn_gate, w_ffn_up,
           w_ffn_down, post_ffn_norm):
    batch, seq, d = x.shape
    depth = w_in.shape[0]
    t = batch * seq
    ssm_w = ssm_a_re.shape[1] * SSM_GROUP
    chunks = seq // SSM_CHUNK
    assert seq % max(CHUNK, SSM_CHUNK) == 0 and batch % 8 == 0
    assert ssm_w == SMALL_W and d % SMALL_W == 0

    w_in_p = _regroup_w_in(w_in)
    bias = jnp.concatenate([jnp.zeros((depth, w_in_p.shape[-1] - b_gate.shape[-1]), F32), b_gate],
                           axis=-1)
    wq = _regroup_w_uq(w_uq)
    wk, wv = _regroup_w_ukv(w_ukv)
    w_o_b, w_glu_b, w_out_b = _to_bf16(w_o_mla), _to_bf16(w_glu), _to_bf16(w_out)
    w_fgu, w_fd = _gate_up_bf16(w_ffn_gate, w_ffn_up), _to_bf16(w_ffn_down)
    ssm_weights = jax.vmap(_ssm_weights)(ssm_a_re, ssm_a_im, ssm_log_dt, ssm_b_re, ssm_b_im,
                                         ssm_c_re, ssm_c_im, ssm_d)

    tab = _rope_table(positions)
    xf = x.reshape(t, d)
    row = lambda v: v.reshape(1, -1)
    for l in range(depth):
        proj, u_cb = _in_proj(xf, row(pre_mix_norm[l]), w_in_p, row(bias[l]),
                              layer=l, batch=batch, seq=seq)
        q, k, v = _qkv_up(proj, tab, row(q_norm[l]), row(kv_norm[l]), wq, wk, wv,
                          layer=l, batch=batch, seq=seq)
        o = _attention(q, k, v).reshape(t, MLA_HEADS * V_HEAD_DIM)

        nb = u_cb.shape[0]
        z_cb = _ssm(u_cb.reshape(nb, chunks * batch * SSM_CHUNK, LANES), ssm_weights,
                    layer=l, batch=batch)
        z_cb = z_cb.reshape(u_cb.shape)

        merged = _merge(o, z_cb, proj, w_o_b, w_glu_b, row(b_glu[l]), layer=l, seq=seq)
        xf = _out_proj(merged, xf, w_out_b, row(post_mix_norm[l]), layer=l)
        xf = _ffn(xf, row(pre_ffn_norm[l]), w_fgu, w_fd, row(post_ffn_norm[l]), layer=l)
    return xf.reshape(batch, seq, d)
```

```python
import functools
import math

import jax
import jax.numpy as jnp
from jax import lax
from jax.experimental import pallas as pl
from jax.experimental.pallas import tpu as pltpu

F32 = jnp.float32
BF16 = jnp.bfloat16

CHUNK = 64
MLA_HEADS = 16
QK_NOPE_DIM = 128
QK_ROPE_DIM = 64
V_HEAD_DIM = 128
Q_LORA_RANK = 512
KV_LORA_RANK = 256
ROPE_THETA = 10000.0
SSM_GROUP = 16
EPS = 1e-6

LANES = 128
HEAD_PAD = 256
SSM_CHUNK = 16
GROUPS_PER_BLOCK = LANES // SSM_GROUP
VMEM_LIMIT = 56 * 1024 * 1024
CAST_BLOCK_BYTES = 8 * 1024 * 1024
NEG_BIG = float(jnp.finfo(jnp.float32).min)

SMALL_W = 1024
OFF_KPE = Q_LORA_RANK + KV_LORA_RANK


def _cparams(*sem):
    return pltpu.CompilerParams(dimension_semantics=sem, vmem_limit_bytes=VMEM_LIMIT)


def _rms(x, w):
    return x * lax.rsqrt(jnp.mean(x * x, axis=-1, keepdims=True) + EPS) * w


def _rope_table_kernel(pos_ref, freq_ref, tab_ref):
    ang = pos_ref[...].astype(F32) * freq_ref[...]
    lane = lax.broadcasted_iota(jnp.int32, ang.shape, 1)
    c = jnp.cos(ang)
    s = jnp.sin(ang)
    tab_ref[...] = jnp.where(lane < 64, c, jnp.where(lane < 96, -s, s))


def _rope_table(positions):
    t = positions.size
    tm = min(t, 2048)
    half = QK_ROPE_DIM // 2
    inv_freq = ROPE_THETA ** (-jnp.arange(0, QK_ROPE_DIM, 2, dtype=F32) / QK_ROPE_DIM)
    freq = jnp.tile(inv_freq, LANES // half).reshape(1, LANES)
    return pl.pallas_call(
        _rope_table_kernel,
        out_shape=jax.ShapeDtypeStruct((t, LANES), F32),
        grid=(t // tm,),
        in_specs=[pl.BlockSpec((tm, 1), lambda i: (i, 0)),
                  pl.BlockSpec((1, LANES), lambda i: (0, 0))],
        out_specs=pl.BlockSpec((tm, LANES), lambda i: (i, 0)),
        compiler_params=_cparams("parallel"),
        name="rope_table",
    )(positions.reshape(t, 1), freq)


def _in_proj_kernel(x_ref, nw_ref, w_ref, b_ref, p_ref, u_ref, h_ref):
    j = pl.program_id(1)

    @pl.when(j == 0)
    def _():
        h_ref[...] = _rms(x_ref[...], nw_ref[...]).astype(BF16)

    acc = jnp.dot(h_ref[...], w_ref[...], preferred_element_type=F32)

    gated = jax.nn.sigmoid(acc + b_ref[...])
    p_ref[...] = jnp.where(j >= 2, gated, acc).astype(BF16)

    @pl.when(j == 0)
    def _():
        u = acc.astype(BF16)
        for g in range(u_ref.shape[0]):
            for c in range(u_ref.shape[1]):
                u_ref[g, c] = u[c * SSM_CHUNK:(c + 1) * SSM_CHUNK, g * LANES:(g + 1) * LANES]


def _in_proj(x, norm_w, w, bias, *, layer, batch, seq):
    t, d = x.shape
    n = w.shape[2]
    tn = SMALL_W
    tm = min(seq, 1024)
    per_b = seq // tm
    cpt = tm // SSM_CHUNK
    nb = tn // LANES
    return pl.pallas_call(
        _in_proj_kernel,
        out_shape=(jax.ShapeDtypeStruct((t, n - tn), BF16),
                   jax.ShapeDtypeStruct((nb, seq // SSM_CHUNK, batch * SSM_CHUNK, LANES), BF16)),
        grid=(t // tm, n // tn),
        in_specs=[pl.BlockSpec((tm, d), lambda i, j: (i, 0)),
                  pl.BlockSpec((1, d), lambda i, j: (0, 0)),
                  pl.BlockSpec((None, d, tn), lambda i, j: (layer, 0, j)),
                  pl.BlockSpec((1, tn), lambda i, j: (0, j))],
        out_specs=(pl.BlockSpec((tm, tn), lambda i, j: (i, jnp.maximum(j - 1, 0))),
                   pl.BlockSpec((nb, cpt, SSM_CHUNK, LANES),
                                lambda i, j: (0, i % per_b, i // per_b, 0))),
        scratch_shapes=[pltpu.VMEM((tm, d), BF16)],
        compiler_params=_cparams("parallel", "arbitrary"),
        name="in_proj",
    )(x, norm_w, w, bias)


def _qkv_up_kernel(p_ref, tab_ref, qn_ref, kvn_ref, wq_ref, wk_ref, wv_ref,
                   q_ref, k_ref, v_ref, *, scale):
    tab = tab_ref[...]
    tabs = tab * scale
    cq = _rms(p_ref[:, 0:Q_LORA_RANK].astype(F32), qn_ref[...]).astype(BF16)
    q_all = jnp.dot(cq, wq_ref[...], preferred_element_type=F32)
    for h in range(MLA_HEADS):
        base = h * HEAD_PAD
        q_ref[h, :, 0:QK_NOPE_DIM] = (q_all[:, base:base + QK_NOPE_DIM] * scale).astype(BF16)
        q_ref[h, :, QK_NOPE_DIM:HEAD_PAD] = (
            q_all[:, base + QK_NOPE_DIM:base + HEAD_PAD] * tabs).astype(BF16)

    ckv = _rms(p_ref[:, Q_LORA_RANK:OFF_KPE].astype(F32), kvn_ref[...]).astype(BF16)
    k_all = jnp.dot(ckv, wk_ref[...], preferred_element_type=F32)
    v_all = jnp.dot(ckv, wv_ref[...], preferred_element_type=F32)
    kt = p_ref[:, OFF_KPE:OFF_KPE + LANES].astype(F32) * tab
    kr = (kt + pltpu.roll(kt, LANES // 2, 1)).astype(BF16)
    ones = jnp.ones((kr.shape[0], LANES), BF16)
    for h in range(MLA_HEADS):
        k_ref[h, :, 0:QK_NOPE_DIM] = k_all[:, h * QK_NOPE_DIM:(h + 1) * QK_NOPE_DIM].astype(BF16)
        k_ref[h, :, QK_NOPE_DIM:HEAD_PAD] = kr
        v_ref[h, :, 0:V_HEAD_DIM] = v_all[:, h * V_HEAD_DIM:(h + 1) * V_HEAD_DIM].astype(BF16)
        v_ref[h, :, V_HEAD_DIM:HEAD_PAD] = ones


def _qkv_up(proj, tab, q_norm, kv_norm, wq, wk, wv, *, layer, batch, seq):
    tm = min(seq, 512)
    nb = seq // tm
    hshape = jax.ShapeDtypeStruct((batch, MLA_HEADS, seq, HEAD_PAD), BF16)
    hspec = pl.BlockSpec((None, MLA_HEADS, tm, HEAD_PAD), lambda b, i: (b, 0, i, 0))
    scale = (QK_NOPE_DIM + QK_ROPE_DIM) ** -0.5 * math.log2(math.e)
    const = lambda b, i: (0, 0)
    return pl.pallas_call(
        functools.partial(_qkv_up_kernel, scale=scale),
        out_shape=(hshape, hshape, hshape),
        grid=(batch, nb),
        in_specs=[pl.BlockSpec((tm, SMALL_W), lambda b, i: (b * nb + i, 0)),
                  pl.BlockSpec((tm, LANES), lambda b, i: (b * nb + i, 0)),
                  pl.BlockSpec((1, Q_LORA_RANK), const),
                  pl.BlockSpec((1, KV_LORA_RANK), const),
                  pl.BlockSpec((None,) + wq.shape[1:], lambda b, i: (layer, 0, 0)),
                  pl.BlockSpec((None,) + wk.shape[1:], lambda b, i: (layer, 0, 0)),
                  pl.BlockSpec((None,) + wv.shape[1:], lambda b, i: (layer, 0, 0))],
        out_specs=(hspec, hspec, hspec),
        compiler_params=_cparams("parallel", "parallel"),
        name="qkv_up",
    )(proj, tab, q_norm, kv_norm, wq, wk, wv)


def _attn_kernel(q_ref, k_ref, v_ref, o_ref, *, tq):
    seq = q_ref.shape[0]
    per = tq // LANES
    row_chunk = lax.broadcasted_iota(jnp.int32, (tq, LANES), 0) // CHUNK
    lane = lax.broadcasted_iota(jnp.int32, (tq, LANES), 1)
    masks = [((c * LANES + lane) // CHUNK) <= row_chunk for c in range(per)]
    nq = seq // tq
    order = [t for pair in zip(range(nq - 1, -1, -1), range(nq)) for t in pair][:nq]
    for i in order:
        kv = (i + 1) * tq
        q = q_ref[i * tq:(i + 1) * tq, :]
        s = lax.dot_general(q, k_ref[0:kv, :], (((1,), (1,)), ((), ())),
                            preferred_element_type=F32)
        pieces = []
        for c in range(kv // LANES):
            piece = s[:, c * LANES:(c + 1) * LANES]
            if c >= i * per:
                piece = jnp.where(masks[c - i * per], piece, NEG_BIG)
            pieces.append(piece)
        mx = pieces[0]
        for piece in pieces[1:]:
            mx = jnp.maximum(mx, piece)
        m = jnp.broadcast_to(jnp.max(mx, axis=1, keepdims=True), (tq, LANES))
        p = jnp.concatenate([jnp.exp2(piece - m) for piece in pieces], axis=1).astype(BF16)
        acc = jnp.dot(p, v_ref[0:kv, :], preferred_element_type=F32)
        o_ref[i * tq:(i + 1) * tq, :] = (
            acc[:, 0:V_HEAD_DIM] / acc[:, V_HEAD_DIM:HEAD_PAD]).astype(o_ref.dtype)


def _attention(q, k, v):
    batch, heads, seq, _ = q.shape
    tq = min(seq, 512)
    spec = pl.BlockSpec((None, None, seq, HEAD_PAD), lambda b, h: (b, h, 0, 0))
    return pl.pallas_call(
        functools.partial(_attn_kernel, tq=tq),
        out_shape=jax.ShapeDtypeStruct((batch, seq, heads * V_HEAD_DIM), BF16),
        grid=(batch, heads),
        in_specs=[spec, spec, spec],
        out_specs=pl.BlockSpec((None, seq, V_HEAD_DIM), lambda b, h: (b, 0, h)),
        compiler_params=_cparams("parallel", "parallel"),
        name="attn",
    )(q, k, v)


def _ssm_weights(a_re, a_im, log_dt, b_re, b_im, c_re, c_im, d_skip):
    L = SSM_CHUNK
    g, n = a_re.shape
    gpb = GROUPS_PER_BLOCK
    nb = g // gpb
    dt = jnp.exp(log_dt)[:, None]
    mag = jnp.exp(a_re * dt)
    abar_re = mag * jnp.cos(a_im * dt)
    abar_im = mag * jnp.sin(a_im * dt)
    den = a_re * a_re + a_im * a_im
    nr = abar_re - 1.0
    f_re = (nr * a_re + abar_im * a_im) / den
    f_im = (abar_im * a_re - nr * a_im) / den
    bb_re = f_re[..., None] * b_re - f_im[..., None] * b_im
    bb_im = f_re[..., None] * b_im + f_im[..., None] * b_re
    kk = jnp.arange(L + 1, dtype=F32)[:, None, None]
    pmag = jnp.exp(kk * (a_re * dt))
    pw_re = pmag * jnp.cos(kk * (a_im * dt))
    pw_im = pmag * jnp.sin(kk * (a_im * dt))
    eye = jnp.eye(gpb, dtype=F32)

    def block_diag(a):
        lead = a.shape[:-3]
        r, c = a.shape[-2:]
        a = a.reshape(lead + (nb, gpb, r, 1, c)) * eye[:, None, :, None]
        return a.reshape(lead + (nb, gpb * r, gpb * c))

    bb_bd =(block_diag(bb_re.transpose(0, 2, 1)), block_diag(bb_im.transpose(0, 2, 1)))
    c_bd = (block_diag(c_re.transpose(0, 2, 1)), block_diag(c_im.transpose(0, 2, 1)))
    pw = jnp.concatenate([pw_re.reshape(L + 1, nb, gpb * n), pw_im.reshape(L + 1, nb, gpb * n)],
                         axis=-1).transpose(1, 0, 2)
    pw_col = jnp.concatenate([pw_re.reshape(L + 1, nb, gpb * n), pw_im.reshape(L + 1, nb, gpb * n)],
                             axis=0).transpose(1, 2, 0)
    d_row = jnp.tile(d_skip.reshape(nb, 1, LANES), (1, 1, L))
    return bb_bd, c_bd, pw, pw_col, d_row


def _steps_to_lanes(u_ref, stage_ref):
    stage_ref[...] = u_ref[...].astype(F32)
    rows = stage_ref.shape[0] // SSM_CHUNK
    return jnp.concatenate([stage_ref[pl.ds(i, rows, stride=SSM_CHUNK), :]
                            for i in range(SSM_CHUNK)], axis=1)


def _ssm_state_kernel(u_ref, bbr_ref, bbi_ref, pw_ref, h_ref, w_ref, stage_ref, v_ref,
                      *, batch, chunks):
    r = pl.program_id(1)

    @pl.when(r == 0)
    def _():
        half = bbr_ref.shape[-1]
        bbr = bbr_ref[...]
        bbi = bbi_ref[...]
        for i in range(SSM_CHUNK):
            k = SSM_CHUNK - 1 - i
            pr = pw_ref[k:k + 1, 0:half]
            pi = pw_ref[k:k + 1, half:]
            w_ref[i * LANES:(i + 1) * LANES, 0:half] = (pr * bbr - pi * bbi).astype(BF16)
            w_ref[i * LANES:(i + 1) * LANES, half:] = (pr * bbi + pi * bbr).astype(BF16)

    u = _steps_to_lanes(u_ref, stage_ref).astype(BF16)
    tr = u.shape[0]
    v_ref[pl.ds(pl.multiple_of(r * tr, tr), tr), :] = jnp.dot(u, w_ref[...],
                                                             preferred_element_type=F32)

    @pl.when(r == pl.num_programs(1) - 1)
    def _():
        half = v_ref.shape[-1] // 2
        ar = jnp.broadcast_to(pw_ref[SSM_CHUNK:SSM_CHUNK + 1, 0:half], (batch, half))
        ai = jnp.broadcast_to(pw_ref[SSM_CHUNK:SSM_CHUNK + 1, half:], (batch, half))

        def step(c, carry):
            hr, hi = carry
            rows = pl.ds(pl.multiple_of(c * batch, batch), batch)
            h_ref[rows, 0:half] = hr
            h_ref[rows, half:] = hi
            return (ar * hr - ai * hi + v_ref[rows, 0:half],
                    ar * hi + ai * hr + v_ref[rows, half:])

        zero = jnp.zeros((batch, half), F32)
        lax.fori_loop(0, chunks, step, (zero, zero))


def _ssm_out_kernel(u_ref, h_ref, bbr_ref, bbi_ref, cr_ref, ci_ref, pw_ref, pc_ref, d_ref, o_ref,
                    t_ref, z_ref, stage_ref):
    @pl.when(pl.program_id(1) == 0)
    def _():
        L = SSM_CHUNK
        half = cr_ref.shape[0]
        bbr = bbr_ref[...]
        bbi = bbi_ref[...]
        cr = cr_ref[...]
        ci = ci_ref[...]
        ab = []
        for tau in range(L):
            pr = pw_ref[tau:tau + 1, 0:half]
            pi = pw_ref[tau:tau + 1, half:]
            ab.append(jnp.concatenate([pr * bbr - pi * bbi, pr * bbi + pi * bbr], axis=1))
        kern = jnp.dot(jnp.concatenate(ab, axis=0).astype(BF16),
                       jnp.concatenate([cr, -ci], axis=0).astype(BF16),
                       preferred_element_type=F32)
        lag = [kern[tau * LANES:(tau + 1) * LANES, :].astype(BF16) for tau in range(L)]
        zero = jnp.zeros((LANES, LANES), BF16)
        for i in range(L):
            for j in range(L):
                t_ref[i * LANES:(i + 1) * LANES, j * LANES:(j + 1) * LANES] = (
                    lag[j - i] if j >= i else zero)
        for j in range(L):
            pr = pc_ref[:, j + 1:j + 2]
            pi = pc_ref[:, L + j + 2:L + j + 3]
            z_ref[0:half, j * LANES:(j + 1) * LANES] = (cr * pr - ci * pi).astype(BF16)
            z_ref[half:, j * LANES:(j + 1) * LANES] = (-(cr * pi + ci * pr)).astype(BF16)

    u = _steps_to_lanes(u_ref, stage_ref)
    ub = u.astype(BF16)
    mid = t_ref.shape[0] // 2
    toep = jnp.concatenate(
        [jnp.dot(ub[:, 0:mid], t_ref[0:mid, 0:mid], preferred_element_type=F32),
         jnp.dot(ub, t_ref[:, mid:], preferred_element_type=F32)], axis=1)
    y = (toep
         + jnp.dot(h_ref[...].astype(BF16), z_ref[...], preferred_element_type=F32)
         + u * d_ref[...])
    z = jax.nn.gelu(y)
    rows = z.shape[0]
    for i in range(SSM_CHUNK):
        stage_ref[pl.ds(i, rows, stride=SSM_CHUNK), :] = z[:, i * LANES:(i + 1) * LANES]
    o_ref[...] = stage_ref[...].astype(o_ref.dtype)


def _ssm(u, weights, *, layer, batch):
    (bb_r, bb_i), (c_r, c_i), pw, pw_col, d_row = weights
    nb, tokens, _ = u.shape
    m = tokens // SSM_CHUNK
    width = SSM_CHUNK * LANES
    states = pw.shape[-1]
    tr = min(m, 512)
    chunks = m // batch
    grow = lambda g, r: (g, r, 0)
    u_spec = pl.BlockSpec((None, tr * SSM_CHUNK, LANES), grow)
    stage = pltpu.VMEM((tr * SSM_CHUNK, LANES), F32)

    def whole(a):
        return pl.BlockSpec((None, None) + a.shape[2:],
                            lambda g, r: (layer, g) + (0,) * (a.ndim - 2))

    h = pl.pallas_call(
        functools.partial(_ssm_state_kernel, batch=batch, chunks=chunks),
        out_shape=jax.ShapeDtypeStruct((nb, m, states), F32),
        grid=(nb, m // tr),
        in_specs=[u_spec, whole(bb_r), whole(bb_i), whole(pw)],
        out_specs=pl.BlockSpec((None, m, states), lambda g, r: (g, 0, 0)),
        scratch_shapes=[pltpu.VMEM((width, states), BF16), stage, pltpu.VMEM((m, states), F32)],
        compiler_params=_cparams("parallel", "arbitrary"),
        name="ssm_state",
    )(u, bb_r, bb_i, pw)

    return pl.pallas_call(
        _ssm_out_kernel,
        out_shape=jax.ShapeDtypeStruct(u.shape, BF16),
        grid=(nb, m // tr),
        in_specs=[u_spec,
                  pl.BlockSpec((None, tr, states), grow),
                  whole(bb_r), whole(bb_i), whole(c_r), whole(c_i), whole(pw), whole(pw_col),
                  whole(d_row)],
        out_specs=u_spec,
        scratch_shapes=[pltpu.VMEM((width, width), BF16), pltpu.VMEM((states, width), BF16),
                        stage],
        compiler_params=_cparams("parallel", "arbitrary"),
        name="ssm_out",
    )(u, h, bb_r, bb_i, c_r, c_i, pw, pw_col, d_row)


def _merge_kernel(o_ref, z_ref, ga_ref, gb_ref, wo_ref, w1_ref, w2_ref, b1_ref, b2_ref, m_ref):
    tm = o_ref.shape[0]
    z = jnp.concatenate([z_ref[g].reshape(tm, LANES) for g in range(z_ref.shape[0])], axis=1)
    z1 = jnp.dot(z, w1_ref[...], preferred_element_type=F32) + b1_ref[...]
    z2 = jnp.dot(z, w2_ref[...], preferred_element_type=F32) + b2_ref[...]
    s = z1 * jax.nn.sigmoid(z2)
    a = jnp.dot(o_ref[...], wo_ref[...], preferred_element_type=F32)
    m_ref[...] = (ga_ref[...].astype(F32) * a + gb_ref[...].astype(F32) * s).astype(m_ref.dtype)


def _merge(o, z_cb, proj, w_o, w_glu, b_glu, *, layer, seq):
    t, d = o.shape
    nb = z_cb.shape[0]
    zw = nb * LANES
    tm = min(seq, 1024)
    per_b = seq // tm
    tn = 512
    nc = d // tn
    ga0 = SMALL_W // tn
    return pl.pallas_call(
        _merge_kernel,
        out_shape=jax.ShapeDtypeStruct((t, d), BF16),
        grid=(t // tm, nc),
        in_specs=[pl.BlockSpec((tm, d), lambda i, j: (i, 0)),
                  pl.BlockSpec((nb, tm // SSM_CHUNK, SSM_CHUNK, LANES),
                               lambda i, j: (0, i % per_b, i // per_b, 0)),
                  pl.BlockSpec((tm, tn), lambda i, j: (i, ga0 + j)),
                  pl.BlockSpec((tm, tn), lambda i, j: (i, ga0 + nc + j)),
                  pl.BlockSpec((None, d, tn), lambda i, j: (layer, 0, j)),
                  pl.BlockSpec((None, zw, tn), lambda i, j: (layer, 0, j)),
                  pl.BlockSpec((None, zw, tn), lambda i, j: (layer, 0, nc + j)),
                  pl.BlockSpec((1, tn), lambda i, j: (0, j)),
                  pl.BlockSpec((1, tn), lambda i, j: (0, nc + j))],
        out_specs=pl.BlockSpec((tm, tn), lambda i, j: (i, j)),
        compiler_params=_cparams("parallel", "arbitrary"),
        name="merge",
    )(o, z_cb, proj, proj, w_o, w_glu, w_glu, b_glu, b_glu)


def _out_proj_kernel(m_ref, x_ref, w_ref, nw_ref, o_ref):
    mix = jnp.dot(m_ref[...], w_ref[...], preferred_element_type=F32)
    o_ref[...] = x_ref[...] + _rms(mix, nw_ref[...])


def _out_proj(merged, x, w_out, norm_w, *, layer):
    t, d = x.shape
    tm = min(t, 512)
    return pl.pallas_call(
        _out_proj_kernel,
        out_shape=jax.ShapeDtypeStruct((t, d), F32),
        grid=(t // tm,),
        in_specs=[pl.BlockSpec((tm, d), lambda i: (i, 0)),
                  pl.BlockSpec((tm, d), lambda i: (i, 0)),
                  pl.BlockSpec((None, d, d), lambda i: (layer, 0, 0)),
                  pl.BlockSpec((1, d), lambda i: (0, 0))],
        out_specs=pl.BlockSpec((tm, d), lambda i: (i, 0)),
        compiler_params=_cparams("parallel"),
        name="out_proj",
    )(merged, x, w_out, norm_w)


def _ffn_kernel(x_ref, nw_ref, wg_ref, wu_ref, wd_ref, pw_ref, o_ref, h_ref, acc_ref):
    j = pl.program_id(1)

    @pl.when(j == 0)
    def _():
        h_ref[...] = _rms(x_ref[...], nw_ref[...]).astype(BF16)
        acc_ref[...] = jnp.zeros(acc_ref.shape, F32)

    h = h_ref[...]
    gate = jnp.dot(h, wg_ref[...], preferred_element_type=F32)
    up = jnp.dot(h, wu_ref[...], preferred_element_type=F32)
    act = (jax.nn.silu(gate) * up).astype(BF16)
    acc_ref[...] += jnp.dot(act, wd_ref[...], preferred_element_type=F32)

    @pl.when(j == pl.num_programs(1) - 1)
    def _():
        o_ref[...] = x_ref[...] + _rms(acc_ref[...], pw_ref[...])


def _ffn(x, pre_w, w_gate, w_up, w_down, post_w, *, layer):
    t, d = x.shape
    f = w_gate.shape[2]
    tm = min(t, 1024)
    tf = 512
    single = pl.Buffered(1)
    return pl.pallas_call(
        _ffn_kernel,
        out_shape=jax.ShapeDtypeStruct((t, d), F32),
        grid=(t // tm, f // tf),
        in_specs=[pl.BlockSpec((tm, d), lambda i, j: (i, 0), pipeline_mode=single),
                  pl.BlockSpec((1, d), lambda i, j: (0, 0)),
                  pl.BlockSpec((None, d, tf), lambda i, j: (layer, 0, j)),
                  pl.BlockSpec((None, d, tf), lambda i, j: (layer, 0, j)),
                  pl.BlockSpec((None, tf, d), lambda i, j: (layer, j, 0)),
                  pl.BlockSpec((1, d), lambda i, j: (0, 0))],
        out_specs=pl.BlockSpec((tm, d), lambda i, j: (i, 0), pipeline_mode=single),
        scratch_shapes=[pltpu.VMEM((tm, d), BF16), pltpu.VMEM((tm, d), F32)],
        compiler_params=_cparams("parallel", "arbitrary"),
        name="ffn",
    )(x, pre_w, w_gate, w_up, w_down, post_w)


def _cast_kernel(x_ref, o_ref):
    o_ref[...] = x_ref[...].astype(o_ref.dtype)


def _to_bf16(w):
    return _regroup_call(_cast_kernel, w, [w.shape[-1]], "to_bf16")[0]


def _regroup_call(body, w, out_cols, name):
    layers, r, c = w.shape
    tr = r
    while tr * c * 4 > CAST_BLOCK_BYTES and tr % 32 == 0:
        tr //= 2
    outs = pl.pallas_call(
        body,
        out_shape=[jax.ShapeDtypeStruct((layers, r, n), BF16) for n in out_cols],
        grid=(layers, r // tr),
        in_specs=[pl.BlockSpec((None, tr, c), lambda l, i: (l, i, 0))],
        out_specs=[pl.BlockSpec((None, tr, n), lambda l, i: (l, i, 0)) for n in out_cols],
        compiler_params=_cparams("parallel", "parallel"),
        name=name,
    )(w)
    return outs


def _w_in_body(x_ref, o_ref):
    off_ssm = OFF_KPE + QK_ROPE_DIM
    half = QK_ROPE_DIM // 2
    gates = o_ref.shape[-1] - 2 * SMALL_W
    off_gate = x_ref.shape[-1] - gates
    ssm_w = off_gate - off_ssm
    x = x_ref[...].astype(BF16)
    o_ref[:, 0:ssm_w] = x[:, off_ssm:off_gate]
    o_ref[:, ssm_w:ssm_w + off_ssm] = x[:, 0:off_ssm]
    base = ssm_w + off_ssm
    o_ref[:, base:base + half] = x[:, OFF_KPE + half:off_ssm]
    o_ref[:, base + half:base + 2 * half] = x[:, OFF_KPE:OFF_KPE + half]
    o_ref[:, base + 2 * half:2 * SMALL_W] = jnp.zeros(
        (x.shape[0], 2 * SMALL_W - base - 2 * half), BF16)
    o_ref[:, 2 * SMALL_W:] = x[:, off_gate:]


def _w_uq_body(x_ref, o_ref):
    x = x_ref[...].astype(BF16)
    qk = QK_NOPE_DIM + QK_ROPE_DIM
    half = QK_ROPE_DIM // 2
    for h in range(MLA_HEADS):
        src = h * qk
        dst = h * HEAD_PAD
        o_ref[:, dst:dst + qk] = x[:, src:src + qk]
        o_ref[:, dst + qk:dst + qk + half] = x[:, src + QK_NOPE_DIM + half:src + qk]
        o_ref[:, dst + qk + half:dst + HEAD_PAD] = x[:, src + QK_NOPE_DIM:src + QK_NOPE_DIM + half]


def _w_ukv_body(x_ref, k_ref, v_ref):
    x = x_ref[...].astype(BF16)
    per = QK_NOPE_DIM + V_HEAD_DIM
    for h in range(MLA_HEADS):
        k_ref[:, h * QK_NOPE_DIM:(h + 1) * QK_NOPE_DIM] = x[:, h * per:h * per + QK_NOPE_DIM]
        v_ref[:, h * V_HEAD_DIM:(h + 1) * V_HEAD_DIM] = x[:, h * per + QK_NOPE_DIM:(h + 1) * per]


def _regroup_w_in(w_in):
    off_ssm = OFF_KPE + QK_ROPE_DIM
    return _regroup_call(_w_in_body, w_in, [w_in.shape[-1] - off_ssm + SMALL_W], "prep_w_in")[0]


def _regroup_w_uq(w_uq):
    return _regroup_call(_w_uq_body, w_uq, [MLA_HEADS * HEAD_PAD], "prep_w_uq")[0]


def _regroup_w_ukv(w_ukv):
    return _regroup_call(_w_ukv_body, w_ukv,
                         [MLA_HEADS * QK_NOPE_DIM, MLA_HEADS * V_HEAD_DIM], "prep_w_ukv")


def kernel(x, positions, pre_mix_norm, w_in, b_gate, q_norm, kv_norm, w_uq, w_ukv, w_o_mla,
           ssm_a_re, ssm_a_im, ssm_log_dt, ssm_b_re, ssm_b_im, ssm_c_re, ssm_c_im, ssm_d,
           w_glu, b_glu, w_out, post_mix_norm, pre_ffn_norm, w_ffn_gate, w_ffn_up,
           w_ffn_down, post_ffn_norm):
    batch, seq, d = x.shape
    depth = w_in.shape[0]
    t = batch * seq
    ssm_w = ssm_a_re.shape[1] * SSM_GROUP
    chunks = seq // SSM_CHUNK
    assert seq % max(CHUNK, SSM_CHUNK) == 0 and batch % 8 == 0
    assert ssm_w == SMALL_W and d % SMALL_W == 0

    w_in_p = _regroup_w_in(w_in)
    bias = jnp.concatenate([jnp.zeros((depth, w_in_p.shape[-1] - b_gate.shape[-1]), F32), b_gate],
                           axis=-1)
    wq = _regroup_w_uq(w_uq)
    wk, wv = _regroup_w_ukv(w_ukv)
    w_o_b, w_glu_b, w_out_b = _to_bf16(w_o_mla), _to_bf16(w_glu), _to_bf16(w_out)
    w_fg, w_fu, w_fd = _to_bf16(w_ffn_gate), _to_bf16(w_ffn_up), _to_bf16(w_ffn_down)
    ssm_weights = jax.vmap(_ssm_weights)(ssm_a_re, ssm_a_im, ssm_log_dt, ssm_b_re, ssm_b_im,
                                         ssm_c_re, ssm_c_im, ssm_d)

    tab = _rope_table(positions)
    xf = x.reshape(t, d)
    row = lambda v: v.reshape(1, -1)
    for l in range(depth):
        proj, u_cb = _in_proj(xf, row(pre_mix_norm[l]), w_in_p, row(bias[l]),
                              layer=l, batch=batch, seq=seq)
        q, k, v = _qkv_up(proj, tab, row(q_norm[l]), row(kv_norm[l]), wq, wk, wv,
                          layer=l, batch=batch, seq=seq)
        o = _attention(q, k, v).reshape(t, MLA_HEADS * V_HEAD_DIM)

        nb = u_cb.shape[0]
        z_cb = _ssm(u_cb.reshape(nb, chunks * batch * SSM_CHUNK, LANES), ssm_weights,
                    layer=l, batch=batch)
        z_cb = z_cb.reshape(u_cb.shape)

        merged = _merge(o, z_cb, proj, w_o_b, w_glu_b, row(b_glu[l]), layer=l, seq=seq)
        xf = _out_proj(merged, xf, w_out_b, row(post_mix_norm[l]), layer=l)
        xf = _ffn(xf, row(pre_ffn_norm[l]), w_fg, w_fu, w_fd, row(post_ffn_norm[l]), layer=l)
    return xf.reshape(batch, seq, d)
```
